```python
import math
import jax
import jax.numpy as jnp
from jax import lax
import numpy as np

D_MODEL = 1024
BATCH = 8
SEQ = 2048
DEPTH = 2
DEC_BATCH = 32
DEC_SEQ = 4
PAST_LEN = 8192
PAGE_SIZE = 128

EPS = 1e-6
NEG = -1e30
LB_FLOOR = 1e-30
D_A = D_MODEL // 2
CONV_W = 31
N_HEADS = 8
N_KV_HEADS = 2
HEAD_DIM = 64
D_B = N_HEADS * HEAD_DIM
KV_DIM = N_KV_HEADS * HEAD_DIM
IDX_HEADS = 4
IDX_DIM = 64
TOPK_MAX = 256
ROPE_THETA = 10000.0
Q_BLOCK = 128
C_HEADS = 4
C_KDIM = 128
C_VDIM = D_MODEL // 2 // C_HEADS
D_C = C_HEADS * C_VDIM
C_FDIM = C_HEADS * C_KDIM
CHUNK = 64

IN_SPLITS = (D_A, D_A, D_A,
             D_B, KV_DIM, KV_DIM, IDX_HEADS * IDX_DIM, IDX_DIM, IDX_HEADS, D_B,
             C_FDIM, C_FDIM, D_C, D_C,
             D_MODEL, D_MODEL, D_MODEL)
N_IN = sum(IN_SPLITS)

kernel_name = 'hybrid_conv_dsa_hgrn2_step'


def _rmsnorm(x, g=None):
    xf = x.astype(jnp.float32)
    y = xf * lax.rsqrt(jnp.mean(xf * xf, axis=-1, keepdims=True) + EPS)
    if g is not None:
        y = y * g.astype(jnp.float32)
    return y.astype(x.dtype)


def _layernorm(x, g, b):
    xf = x.astype(jnp.float32)
    mu = jnp.mean(xf, axis=-1, keepdims=True)
    var = jnp.mean(jnp.square(xf - mu), axis=-1, keepdims=True)
    y = (xf - mu) * lax.rsqrt(var + EPS) * g.astype(jnp.float32) + b.astype(jnp.float32)
    return y.astype(x.dtype)


def _rope(x, pos):
    half = x.shape[-1] // 2
    inv = ROPE_THETA ** (-jnp.arange(half, dtype=jnp.float32) / half)
    ang = pos.astype(jnp.float32)[:, None] * inv[None, :]
    cos = jnp.cos(ang)[None, :, None, :]
    sin = jnp.sin(ang)[None, :, None, :]
    xf = x.astype(jnp.float32)
    x1, x2 = xf[..., :half], xf[..., half:]
    return jnp.concatenate([x1 * cos - x2 * sin, x2 * cos + x1 * sin], axis=-1).astype(x.dtype)


def _split_in(z):
    offs = np.cumsum(IN_SPLITS)[:-1].tolist()
    return jnp.split(z, offs, axis=-1)


def _gather_rows(a, idx):
    return jax.vmap(lambda ab, ib: ab[ib])(a, idx)


def _indexer_scores(qi, ki, wi):
    s = jnp.einsum('bqhd,bsd->bqhs', qi, ki).astype(jnp.float32) * (IDX_DIM ** -0.5)
    return jnp.einsum('bqhs,bqh->bqs', jax.nn.relu(s), wi.astype(jnp.float32)) * (IDX_HEADS ** -0.5)


def _sparse_attend(q, k_sel, v_sel, valid):
    B, T = q.shape[:2]
    qg = q.reshape(B, T, N_KV_HEADS, N_HEADS // N_KV_HEADS, HEAD_DIM)
    s = jnp.einsum('btngd,btsnd->btngs', qg, k_sel).astype(jnp.float32) * (HEAD_DIM ** -0.5)
    s = jnp.where(valid[:, :, None, None, :], s, NEG)
    pr = jax.nn.softmax(s, axis=-1).astype(v_sel.dtype)
    o = jnp.einsum('btngs,btsnd->btngd', pr, v_sel)
    return o.reshape(B, T, D_B)


def _dsa_prompt(q, k, v, qi, ki, wi):
    B, T = q.shape[:2]
    topk = min(TOPK_MAX, T // 4)
    nb = T // Q_BLOCK
    key_pos = jnp.arange(T)

    def blk(xs):
        qb, qib, wib, qpos = xs
        scores = _indexer_scores(qib, ki, wib)
        scores = jnp.where(key_pos[None, None, :] <= qpos[None, :, None], scores, NEG)
        _, idx = lax.top_k(scores, topk)
        valid = idx <= qpos[None, :, None]
        return _sparse_attend(qb, _gather_rows(k, idx), _gather_rows(v, idx), valid)

    def to_blocks(a):
        return a.reshape(B, nb, Q_BLOCK, *a.shape[2:]).swapaxes(0, 1)

    out = lax.map(blk, (to_blocks(q), to_blocks(qi), to_blocks(wi), key_pos.reshape(nb, Q_BLOCK)))
    return out.swapaxes(0, 1).reshape(B, T, D_B)


def _dsa_sample(q, k, v, qi, ki, wi, ck, cv, cki, page_table):
    B, T = q.shape[:2]
    past_len = page_table.shape[1] * PAGE_SIZE
    L = past_len + T
    topk = min(TOPK_MAX, L // 4)
    ki_past = cki[page_table].reshape(B, past_len, IDX_DIM).astype(ki.dtype)
    ki_all = jnp.concatenate([ki_past, ki], axis=1)
    qpos = past_len + jnp.arange(T)
    scores = _indexer_scores(qi, ki_all, wi)
    scores = jnp.where(jnp.arange(L)[None, None, :] <= qpos[None, :, None], scores, NEG)
    _, idx = lax.top_k(scores, topk)
    valid = idx <= qpos[None, :, None]
    is_new = (idx >= past_len)[..., None, None]
    idx_past = jnp.minimum(idx, past_len - 1)
    rows = _gather_rows(page_table, idx_past // PAGE_SIZE) * PAGE_SIZE + idx_past % PAGE_SIZE
    idx_new = jnp.clip(idx - past_len, 0, T - 1)
    ck_flat = ck.reshape(-1, N_KV_HEADS, HEAD_DIM)
    cv_flat = cv.reshape(-1, N_KV_HEADS, HEAD_DIM)
    k_sel = jnp.where(is_new, _gather_rows(k, idx_new), ck_flat[rows].astype(k.dtype))
    v_sel = jnp.where(is_new, _gather_rows(v, idx_new), cv_flat[rows].astype(v.dtype))
    return _sparse_attend(q, k_sel, v_sel, valid)


def _conv_module(u, buf, w_dw, b_dw, ln_g, ln_b):
    full = jnp.concatenate([buf.astype(u.dtype), u], axis=1)
    y = lax.conv_general_dilated(full, w_dw[:, None, :].astype(full.dtype), window_strides=(1,),
                                 padding='VALID', dimension_numbers=('NWC', 'WIO', 'NWC'),
                                 feature_group_count=u.shape[-1])
    y = jax.nn.silu(_layernorm(y + b_dw, ln_g, ln_b))
    return y, full[:, -(CONV_W - 1):]


def _hgrn2(q, log_f, kk, i, S0):
    B, T, H, dk = q.shape
    dv = i.shape[-1]
    C = math.gcd(T, CHUNK)
    N = T // C

    def chunks(a):
        return a.astype(jnp.float32).reshape(B, N, C, H, a.shape[-1]).transpose(1, 0, 3, 2, 4)

    causal = jnp.tril(jnp.ones((C, C), dtype=bool))

    def step(S, xs):
        qc, lfc, kc, ic = xs
        b = jnp.cumsum(lfc, axis=2)
        inter = jnp.einsum('bhtd,bhdv->bhtv', qc * jnp.exp(b), S)
        diff = b[:, :, :, None, :] - b[:, :, None, :, :]
        decay = jnp.exp(jnp.where(causal[None, None, :, :, None], diff, NEG))
        att = jnp.einsum('bhtd,bhtsd,bhsd->bhts', qc, decay, kc)
        o = inter + jnp.einsum('bhts,bhsv->bhtv', att, ic)
        b_end = b[:, :, -1:, :]
        S_new = jnp.exp(b_end[:, :, 0, :])[..., None] * S + jnp.einsum(
            'bhsd,bhsv->bhdv', kc * jnp.exp(b_end - b), ic)
        return S_new, o

    S, o = lax.scan(step, S0.astype(jnp.float32), (chunks(q), chunks(log_f), chunks(kk), chunks(i)))
    o = o.transpose(1, 0, 3, 2, 4).reshape(B, T, H, dv)
    return o, S


def _layer(x, c, pos, p, lb, conv_buf, S0, attend):
    B, T, _ = x.shape
    mod = jax.nn.silu(c) @ p['w_ada'] + p['b_ada']
    shift, scale, gate = jnp.split(mod, 3, axis=-1)
    h = _rmsnorm(x, p['norm_g']) * (1 + scale[:, None, :]) + shift[:, None, :]
    (a_val, a_glu, a_gate, zq, zk, zv, zqi, zki, zwi, b_gate,
     cq, cf, ci, c_gate, g_a, g_b, g_c) = _split_in(h @ p['w_in'])

    u = a_val * jax.nn.sigmoid(a_glu)
    ya, new_buf = _conv_module(u, conv_buf, p['w_dw'], p['b_dw'], p['ln_g'], p['ln_b'])
    ya = ya * jax.nn.silu(a_gate)

    q = _rope(_rmsnorm(zq.reshape(B, T, N_HEADS, HEAD_DIM), p['q_norm_g']), pos)
    k = _rope(_rmsnorm(zk.reshape(B, T, N_KV_HEADS, HEAD_DIM), p['k_norm_g']), pos)
    v = zv.reshape(B, T, N_KV_HEADS, HEAD_DIM)
    qi = _rope(zqi.reshape(B, T, IDX_HEADS, IDX_DIM), pos)
    ki = _rope(_rmsnorm(zki)[:, :, None, :], pos)[:, :, 0, :]
    yb = attend(q, k, v, qi, ki, zwi) * jax.nn.silu(b_gate)

    fx = cf.reshape(B, T, C_HEADS, C_KDIM).astype(jnp.float32)
    log_f = jnp.logaddexp(jnp.log(jnp.maximum(lb, LB_FLOOR)), jnp.log1p(-lb) + jax.nn.log_sigmoid(fx))
    kk = (1.0 - lb) * jax.nn.sigmoid(-fx)
    qc = jax.nn.silu(cq).reshape(B, T, C_HEADS, C_KDIM)
    o, S_new = _hgrn2(qc, log_f, kk, ci.reshape(B, T, C_HEADS, C_VDIM), S0)
    yc = _rmsnorm(o, p['c_norm_g']).astype(x.dtype).reshape(B, T, D_C) * jax.nn.silu(c_gate)

    m = (jax.nn.sigmoid(g_a) * (ya @ p['w_proj_a'])
         + jax.nn.sigmoid(g_b) * (yb @ p['w_proj_b'])
         + jax.nn.sigmoid(g_c) * (yc @ p['w_proj_c']))
    y = x + gate[:, None, :] * (m @ p['w_out'])
    return y, k, v, ki, new_buf, S_new


def setup_inputs(seed: int = 0) -> dict:
    key = jax.random.key(seed)
    ks = jax.random.split(key, 32)
    n_pages = PAST_LEN // PAGE_SIZE
    n_pool = (DEC_BATCH * n_pages * 5) // 4

    def nrm(k, shape, s):
        return jax.random.normal(k, shape, jnp.float32) * s

    page_table = jax.random.permutation(ks[7], n_pool)[: DEC_BATCH * n_pages]
    page_table = page_table.reshape(DEC_BATCH, n_pages).astype(jnp.int32)
    return {
        'x_prompt': nrm(ks[0], (BATCH, SEQ, D_MODEL), 1.0),
        'x_sample': nrm(ks[1], (DEC_BATCH, DEC_SEQ, D_MODEL), 1.0),
        'cache_k': nrm(ks[2], (DEPTH, n_pool, PAGE_SIZE, N_KV_HEADS, HEAD_DIM), 1.0),
        'cache_v': nrm(ks[3], (DEPTH, n_pool, PAGE_SIZE, N_KV_HEADS, HEAD_DIM), 1.0),
        'cache_idx_k': nrm(ks[4], (DEPTH, n_pool, PAGE_SIZE, IDX_DIM), 1.0),
        'state_conv': nrm(ks[5], (DEPTH, DEC_BATCH, CONV_W - 1, D_A), 0.5),
        'state_hgrn': nrm(ks[6], (DEPTH, DEC_BATCH, C_HEADS, C_KDIM, C_VDIM), 0.5),
        'page_table': page_table,
        'c_prompt': nrm(ks[8], (BATCH, D_MODEL), 1.0),
        'c_sample': nrm(ks[9], (DEC_BATCH, D_MODEL), 1.0),
        'w_ada': nrm(ks[10], (DEPTH, D_MODEL, 3 * D_MODEL), 0.5 * D_MODEL ** -0.5),
        'b_ada': nrm(ks[11], (DEPTH, 3 * D_MODEL), 0.02),
        'norm_g': 1.0 + nrm(ks[12], (DEPTH, D_MODEL), 0.02),
        'w_in': nrm(ks[13], (DEPTH, D_MODEL, N_IN), D_MODEL ** -0.5),
        'w_dw': nrm(ks[14], (DEPTH, CONV_W, D_A), CONV_W ** -0.5),
        'b_dw': nrm(ks[15], (DEPTH, D_A), 0.02),
        'ln_g': 1.0 + nrm(ks[16], (DEPTH, D_A), 0.02),
        'ln_b': nrm(ks[17], (DEPTH, D_A), 0.02),
        'q_norm_g': 1.0 + nrm(ks[18], (DEPTH, HEAD_DIM), 0.02),
        'k_norm_g': 1.0 + nrm(ks[19], (DEPTH, HEAD_DIM), 0.02),
        'lb_logits': nrm(ks[20], (DEPTH, C_FDIM), 0.5),
        'c_norm_g': 1.0 + nrm(ks[21], (DEPTH, C_VDIM), 0.02),
        'w_proj_a': nrm(ks[22], (DEPTH, D_A, D_MODEL), D_A ** -0.5),
        'w_proj_b': nrm(ks[23], (DEPTH, D_B, D_MODEL), D_B ** -0.5),
        'w_proj_c': nrm(ks[24], (DEPTH, D_C, D_MODEL), D_C ** -0.5),
        'w_out': nrm(ks[25], (DEPTH, D_MODEL, D_MODEL), D_MODEL ** -0.5),
    }


def reference(x_prompt, x_sample, cache_k, cache_v, cache_idx_k, state_conv, state_hgrn, page_table,
              c_prompt, c_sample, w_ada, b_ada, norm_g, w_in, w_dw, b_dw, ln_g, ln_b, q_norm_g, k_norm_g,
              lb_logits, c_norm_g, w_proj_a, w_proj_b, w_proj_c, w_out):
    lbp = jax.nn.softmax(lb_logits.astype(jnp.float32), axis=0)
    lb_all = jnp.cumsum(lbp, axis=0) - lbp[0:1]
    Bp, Tp = x_prompt.shape[:2]
    pos_p = jnp.arange(Tp)
    pos_s = page_table.shape[1] * PAGE_SIZE + jnp.arange(x_sample.shape[1])
    buf0 = jnp.zeros((Bp, CONV_W - 1, D_A), x_prompt.dtype)
    S00 = jnp.zeros((Bp, C_HEADS, C_KDIM, C_VDIM), jnp.float32)

    xp, xs = x_prompt, x_sample
    kp_l, vp_l, kip_l, bp_l, sp_l = [], [], [], [], []
    ks_l, vs_l, kis_l, bs_l, ss_l = [], [], [], [], []
    for l in range(DEPTH):
        p = {'w_ada': w_ada[l], 'b_ada': b_ada[l], 'norm_g': norm_g[l], 'w_in': w_in[l],
             'w_dw': w_dw[l], 'b_dw': b_dw[l], 'ln_g': ln_g[l], 'ln_b': ln_b[l],
             'q_norm_g': q_norm_g[l], 'k_norm_g': k_norm_g[l], 'c_norm_g': c_norm_g[l],
             'w_proj_a': w_proj_a[l], 'w_proj_b': w_proj_b[l], 'w_proj_c': w_proj_c[l], 'w_out': w_out[l]}
        lb = lb_all[l].reshape(C_HEADS, C_KDIM)

        xp, kp, vp, kip, bp, sp = _layer(xp, c_prompt, pos_p, p, lb, buf0, S00, _dsa_prompt)

        def attend_s(q, k, v, qi, ki, wi, l=l):
            return _dsa_sample(q, k, v, qi, ki, wi, cache_k[l], cache_v[l], cache_idx_k[l], page_table)

        xs, ksm, vsm, kism, bsm, ssm = _layer(xs, c_sample, pos_s, p, lb, state_conv[l], state_hgrn[l], attend_s)

        kp_l.append(kp); vp_l.append(vp); kip_l.append(kip); bp_l.append(bp); sp_l.append(sp.astype(x_prompt.dtype))
        ks_l.append(ksm); vs_l.append(vsm); kis_l.append(kism); bs_l.append(bsm); ss_l.append(ssm.astype(state_hgrn.dtype))

    k_prompt = jnp.stack(kp_l)
    v_prompt = jnp.stack(vp_l)
    idxk_prompt = jnp.stack(kip_l)
    conv_prompt = jnp.stack(bp_l)
    hgrn_prompt = jnp.stack(sp_l)
    k_sample = jnp.stack(ks_l)
    v_sample = jnp.stack(vs_l)
    idxk_sample = jnp.stack(kis_l)
    conv_sample = jnp.stack(bs_l)
    hgrn_sample = jnp.stack(ss_l)
    return (xp, xs, k_prompt, v_prompt, idxk_prompt, conv_prompt, hgrn_prompt,
            k_sample, v_sample, idxk_sample, conv_sample, hgrn_sample)
```

```python
import functools
import math

import jax
import jax.numpy as jnp
import numpy as np
from jax import lax
from jax.experimental import pallas as pl
from jax.experimental.pallas import tpu as pltpu

F32 = jnp.float32
BF16 = jnp.bfloat16
I32 = jnp.int32

LANES = 128
SUBLANES = 8
VMEM_LIMIT = 56 * 1024 * 1024

D_MODEL = 1024
EPS = 1e-6
NEG = -1e30
LB_FLOOR = 1e-30
D_A = D_MODEL // 2
CONV_W = 31
HALO = 32
N_HEADS = 8
N_KV_HEADS = 2
HEAD_DIM = 64
D_B = N_HEADS * HEAD_DIM
KV_DIM = N_KV_HEADS * HEAD_DIM
IDX_HEADS = 4
IDX_DIM = 64
TOPK_MAX = 256
ROPE_THETA = 10000.0
Q_BLOCK = 128
C_HEADS = 4
C_KDIM = 128
C_VDIM = D_MODEL // 2 // C_HEADS
D_C = C_HEADS * C_VDIM
C_FDIM = C_HEADS * C_KDIM
CHUNK = 64
PAGE_SIZE = 128

_IN_NAMES = ("a_val", "a_glu", "a_gate", "zq", "zk", "zv", "zqi", "zki", "zwi", "b_gate",
             "cq", "cf", "ci", "c_gate", "g_a", "g_b", "g_c")
_IN_WIDTHS = (D_A, D_A, D_A, D_B, KV_DIM, KV_DIM, IDX_HEADS * IDX_DIM, IDX_DIM, IDX_HEADS, D_B,
              C_FDIM, C_FDIM, D_C, D_C, D_MODEL, D_MODEL, D_MODEL)
_IN_OFFS = dict(zip(_IN_NAMES, np.concatenate([[0], np.cumsum(_IN_WIDTHS)[:-1]]).tolist()))
_IN_W = dict(zip(_IN_NAMES, _IN_WIDTHS))
_PACK_ORDER = ("g_a", "g_b", "g_c", "a_val", "a_glu", "a_gate", "zq", "b_gate", "cq", "cf", "ci",
               "c_gate", "zqi", "zk", "zv", "zki", "zwi")


def _round_up(n, m):
    return (n + m - 1) // m * m


_PACK_W = {n: _round_up(_IN_W[n], LANES) for n in _PACK_ORDER}
_PACK_OFF = {}
_o = 0
for _n in _PACK_ORDER:
    assert _o % _PACK_W[_n] == 0
    _PACK_OFF[_n] = _o
    _o += _PACK_W[_n]
NZ = _o


def _pack_w_in(w):
    parts = []
    for n in _PACK_ORDER:
        seg = w[:, _IN_OFFS[n]:_IN_OFFS[n] + _IN_W[n]]
        pad = _PACK_W[n] - _IN_W[n]
        if pad:
            seg = jnp.pad(seg, ((0, 0), (0, pad)))
        parts.append(seg)
    return jnp.concatenate(parts, axis=1)


def _col(name, width):
    assert _PACK_OFF[name] % width == 0
    return _PACK_OFF[name] // width


def _sigmoid(x):
    return jax.nn.sigmoid(x)


def _silu(x):
    return x * jax.nn.sigmoid(x)


def _params(sem):
    return pltpu.CompilerParams(dimension_semantics=sem, vmem_limit_bytes=VMEM_LIMIT)


def _ada_kernel(c_ref, w_ref, b_ref, o_ref):
    c = c_ref[...]
    o_ref[...] = jnp.dot(_silu(c).astype(BF16), w_ref[...], preferred_element_type=F32) + b_ref[...]


def _ada(c, w_bf, b):
    n, d = c.shape
    nout = w_bf.shape[1]
    tn = D_MODEL
    return pl.pallas_call(
        _ada_kernel,
        grid=(nout // tn,),
        in_specs=[pl.BlockSpec((n, d), lambda j: (0, 0)),
                  pl.BlockSpec((d, tn), lambda j: (0, j)),
                  pl.BlockSpec((1, tn), lambda j: (0, j))],
        out_specs=pl.BlockSpec((n, tn), lambda j: (0, j)),
        out_shape=jax.ShapeDtypeStruct((n, nout), F32),
        compiler_params=_params(("arbitrary",)),
        name="ada",
    )(c, w_bf, b.reshape(1, nout))


def _prenorm_kernel(x_ref, g_ref, sc_ref, sh_ref, o_ref):
    x = x_ref[...]
    ms = jnp.mean(x * x, axis=-1, keepdims=True)
    y = x * lax.rsqrt(ms + EPS) * g_ref[...]
    o_ref[...] = (y * (1.0 + sc_ref[0]) + sh_ref[0]).astype(o_ref.dtype)


def _prenorm(x2, g, scale3, shift3, tm, tiles_per_mod):
    m, d = x2.shape
    r = scale3.shape[1]
    return pl.pallas_call(
        _prenorm_kernel,
        grid=(m // tm,),
        in_specs=[pl.BlockSpec((tm, d), lambda i: (i, 0)),
                  pl.BlockSpec((1, d), lambda i: (0, 0)),
                  pl.BlockSpec((1, r, d), lambda i: (i // tiles_per_mod, 0, 0)),
                  pl.BlockSpec((1, r, d), lambda i: (i // tiles_per_mod, 0, 0))],
        out_specs=pl.BlockSpec((tm, d), lambda i: (i, 0)),
        out_shape=jax.ShapeDtypeStruct((m, d), BF16),
        compiler_params=_params(("parallel",)),
        name="prenorm",
    )(x2, g.reshape(1, d), scale3, shift3)


def _mm_kernel(a_ref, b_ref, o_ref):
    o_ref[...] = jnp.dot(a_ref[...], b_ref[...], preferred_element_type=F32)


def _matmul(a_bf, b_bf, tm, tn):
    m, k = a_bf.shape
    n = b_bf.shape[1]
    return pl.pallas_call(
        _mm_kernel,
        grid=(n // tn, m // tm),
        in_specs=[pl.BlockSpec((tm, k), lambda j, i: (i, 0)),
                  pl.BlockSpec((k, tn), lambda j, i: (0, j))],
        out_specs=pl.BlockSpec((tm, tn), lambda j, i: (i, j)),
        out_shape=jax.ShapeDtypeStruct((m, n), F32),
        compiler_params=_params(("parallel", "parallel")),
        name="inproj",
    )(a_bf, b_bf)


def _ln_swish_gate(y, gate, bdw, lng, lnb):
    y = y + bdw
    mu = jnp.mean(y, axis=-1, keepdims=True)
    yc = y - mu
    var = jnp.mean(yc * yc, axis=-1, keepdims=True)
    yn = yc * lax.rsqrt(var + EPS) * lng + lnb
    return _silu(yn) * _silu(gate)


def _conv_prompt_kernel(val_ref, glu_ref, gate_ref, valh_ref, gluh_ref, buf_ref, w_ref, bdw_ref,
                        lng_ref, lnb_ref, ya_ref, utail_ref, f_ref, *, tt):
    ti = pl.program_id(1)
    u = val_ref[0] * _sigmoid(glu_ref[0])
    uh = valh_ref[0] * _sigmoid(gluh_ref[0])
    f_ref[0:HALO, :] = jnp.where(ti == 0, buf_ref[0], uh)
    f_ref[HALO:HALO + tt, :] = u
    off = HALO - (CONV_W - 1)
    acc = jnp.zeros((tt, D_A), F32)
    for j in range(CONV_W):
        acc = acc + w_ref[j:j + 1, :] * f_ref[pl.ds(off + j, tt), :]
    ya_ref[0] = _ln_swish_gate(acc, gate_ref[0], bdw_ref[...], lng_ref[...], lnb_ref[...])

    @pl.when(ti == pl.num_programs(1) - 1)
    def _():
        utail_ref[0] = f_ref[tt:tt + HALO, :]


def _conv_prompt(z3, buf32, w_dw, b_dw, ln_g, ln_b, tt):
    b, t, _ = z3.shape
    hb = tt // HALO
    cur = lambda name: pl.BlockSpec((1, tt, D_A), lambda bi, ti, c=_col(name, D_A): (bi, ti, c))
    halo = lambda name: pl.BlockSpec(
        (1, HALO, D_A), lambda bi, ti, c=_col(name, D_A): (bi, jnp.maximum(ti * hb - 1, 0), c))
    vec = pl.BlockSpec((1, D_A), lambda bi, ti: (0, 0))
    wpad = jnp.pad(w_dw, ((0, HALO - CONV_W), (0, 0)))
    return pl.pallas_call(
        functools.partial(_conv_prompt_kernel, tt=tt),
        grid=(b, t // tt),
        in_specs=[cur("a_val"), cur("a_glu"), cur("a_gate"), halo("a_val"), halo("a_glu"),
                  pl.BlockSpec((1, HALO, D_A), lambda bi, ti: (bi, 0, 0)),
                  pl.BlockSpec((HALO, D_A), lambda bi, ti: (0, 0)), vec, vec, vec],
        out_specs=[pl.BlockSpec((1, tt, D_A), lambda bi, ti: (bi, ti, 0)),
                   pl.BlockSpec((1, HALO, D_A), lambda bi, ti: (bi, 0, 0))],
        out_shape=[jax.ShapeDtypeStruct((b, t, D_A), F32),
                   jax.ShapeDtypeStruct((b, HALO, D_A), F32)],
        scratch_shapes=[pltpu.VMEM((HALO + tt, D_A), F32)],
        compiler_params=_params(("parallel", "arbitrary")),
        name="conv_prompt",
    )(z3, z3, z3, z3, z3, buf32, wpad, b_dw.reshape(1, D_A), ln_g.reshape(1, D_A), ln_b.reshape(1, D_A))


def _conv_sample_kernel(val_ref, glu_ref, gate_ref, buf_ref, w_ref, bdw_ref, lng_ref, lnb_ref,
                        ya_ref, u_ref, *, ts):
    nb = CONV_W - 1
    for t in range(ts):
        u_ref[t] = val_ref[t] * _sigmoid(glu_ref[t])
    for t in range(ts):
        acc = jnp.zeros(u_ref.shape[1:], F32)
        for j in range(CONV_W):
            r = t + j
            src = buf_ref[r] if r < nb else u_ref[r - nb]
            acc = acc + w_ref[j:j + 1, :] * src
        ya_ref[t] = _ln_swish_gate(acc, gate_ref[t], bdw_ref[...], lng_ref[...], lnb_ref[...])


def _conv_sample(val_t, glu_t, gate_t, buf_t, w_dw, b_dw, ln_g, ln_b):
    ts, b, _ = val_t.shape
    assert ts <= CONV_W - 1
    return pl.pallas_call(
        functools.partial(_conv_sample_kernel, ts=ts),
        out_shape=[jax.ShapeDtypeStruct((ts, b, D_A), F32), jax.ShapeDtypeStruct((ts, b, D_A), F32)],
        compiler_params=pltpu.CompilerParams(vmem_limit_bytes=VMEM_LIMIT),
        name="conv_sample",
    )(val_t, glu_t, gate_t, buf_t, w_dw, b_dw.reshape(1, D_A), ln_g.reshape(1, D_A), ln_b.reshape(1, D_A))


def _lane_iota(shape):
    return lax.broadcasted_iota(I32, shape, len(shape) - 1)


def _swap_half(x):
    w = x.shape[-1]
    half = HEAD_DIM // 2
    first = (_lane_iota(x.shape) % HEAD_DIM) < half
    return jnp.where(first, pltpu.roll(x, w - half, 1), pltpu.roll(x, half, 1))


def _group_sum(s):
    w = s.shape[-1]
    lane = _lane_iota(s.shape)
    sh = HEAD_DIM // 2
    while sh >= 1:
        partner = jnp.where((lane % (2 * sh)) < sh, pltpu.roll(s, w - sh, 1), pltpu.roll(s, sh, 1))
        s = s + partner
        sh //= 2
    return s


def _tile_lanes(t, w):
    reps = w // t.shape[-1]
    return t if reps == 1 else jnp.concatenate([t] * reps, axis=-1)


def _rope(x, cos, sin):
    w = x.shape[-1]
    return x * _tile_lanes(cos, w) + _swap_half(x) * _tile_lanes(sin, w)


def _head_rms(x, g):
    ms = _group_sum(x * x) * (1.0 / HEAD_DIM)
    y = x * lax.rsqrt(ms + EPS)
    return y if g is None else y * _tile_lanes(g, x.shape[-1])


def _qkrope_kernel(zq_ref, zk_ref, zqi_ref, zki_ref, zv_ref, cos_ref, sin_ref, qg_ref, kg_ref,
                   q_ref, k_ref, qi_ref, ki_ref, v_ref, kb_ref, vb_ref, kib_ref):
    cos = cos_ref[...]
    sin = sin_ref[...]
    q_ref[...] = _rope(_head_rms(zq_ref[...], qg_ref[...]), cos, sin)
    k = _rope(_head_rms(zk_ref[...], kg_ref[...]), cos, sin)
    k_ref[...] = k
    kb_ref[...] = k.astype(BF16)
    qi_ref[...] = _rope(zqi_ref[...], cos, sin)
    ki = _rope(_head_rms(zki_ref[...], None), cos, sin)[:, :IDX_DIM]
    ki_ref[...] = ki
    kib_ref[...] = ki.astype(BF16)
    v = zv_ref[...]
    v_ref[...] = v
    vb_ref[...] = v.astype(BF16)


def _qkrope(z, cos_t, sin_t, qg, kg, tm, table_tiles):
    m = z.shape[0]
    zc = lambda name, w: pl.BlockSpec((tm, w), lambda i, c=_col(name, w): (i, c))
    tab = pl.BlockSpec((tm, LANES), lambda i: (i % table_tiles, 0))
    vec = pl.BlockSpec((1, LANES), lambda i: (0, 0))
    row = lambda w: pl.BlockSpec((tm, w), lambda i: (i, 0))
    qi_w = IDX_HEADS * IDX_DIM
    return pl.pallas_call(
        _qkrope_kernel,
        grid=(m // tm,),
        in_specs=[zc("zq", D_B), zc("zk", KV_DIM), zc("zqi", qi_w), zc("zki", LANES), zc("zv", KV_DIM),
                  tab, tab, vec, vec],
        out_specs=[row(D_B), row(KV_DIM), row(qi_w), row(IDX_DIM), row(KV_DIM),
                   row(KV_DIM), row(KV_DIM), row(IDX_DIM)],
        out_shape=[jax.ShapeDtypeStruct((m, D_B), F32), jax.ShapeDtypeStruct((m, KV_DIM), F32),
                   jax.ShapeDtypeStruct((m, qi_w), F32), jax.ShapeDtypeStruct((m, IDX_DIM), F32),
                   jax.ShapeDtypeStruct((m, KV_DIM), F32),
                   jax.ShapeDtypeStruct((m, KV_DIM), BF16), jax.ShapeDtypeStruct((m, KV_DIM), BF16),
                   jax.ShapeDtypeStruct((m, IDX_DIM), BF16)],
        compiler_params=_params(("parallel",)),
        name="qkrope",
    )(z, z, z, z, z, cos_t, sin_t, qg, kg)


def _rope_tables(pos):
    half = HEAD_DIM // 2
    inv = ROPE_THETA ** (-jnp.arange(half, dtype=F32) / half)
    ang = pos.astype(F32)[:, None] * inv[None, :]
    cos = jnp.cos(ang)
    sin = jnp.sin(ang)
    cos64 = jnp.concatenate([cos, cos], axis=1)
    sin64 = jnp.concatenate([-sin, sin], axis=1)
    reps = LANES // HEAD_DIM
    return jnp.tile(cos64, (1, reps)), jnp.tile(sin64, (1, reps))


_SIGN = np.int32(-2 ** 31)
_MAG = np.int32(0x7FFFFFFF)
SEARCH_UNROLL = 4


def _dot_nt(a, b):
    return lax.dot_general(a, b, (((1,), (1,)), ((), ())), preferred_element_type=F32)


def _row_count(mask):
    return jnp.sum(mask.astype(F32), axis=1, keepdims=True)


def _ordered_to_float(t):
    key = t ^ _SIGN
    return lax.bitcast_convert_type(jnp.where(key < 0, key ^ _MAG, key), F32)


def _select_topk(sc_ref, sel_ref, topk):
    nq, nl = sc_ref.shape
    if nl == topk:
        sel_ref[...] = jnp.ones((nq, nl), F32)
        return
    kf = float(topk)

    def cond(c):
        i, _, cnt_t = c
        return (i < 32) & (jnp.max(jnp.abs(cnt_t - kf)) > 0.0)

    def body(c):
        i, t, cnt_t = c
        for j in range(SEARCH_UNROLL):
            cand = t | jnp.left_shift(jnp.int32(1), 31 - (i + j))
            cnt = _row_count(sc_ref[...] >= _ordered_to_float(cand))
            ok = cnt >= kf
            t = jnp.where(ok, cand, t)
            cnt_t = jnp.where(ok, cnt, cnt_t)
        return i + SEARCH_UNROLL, t, cnt_t

    init = (jnp.int32(0), jnp.zeros((nq, 1), I32), jnp.full((nq, 1), float(nl), F32))
    _, t, _ = lax.while_loop(cond, body, init)
    tau = jnp.where(t == 0, -jnp.inf, _ordered_to_float(t))
    sc = sc_ref[...]
    gt = sc > tau
    eq = sc == tau
    need = kf - _row_count(gt)
    sel_ref[...] = (gt | eq).astype(F32)
    tie_rows = (_row_count(eq) > need) & (tau > NEG)

    @pl.when(jnp.max(tie_rows.astype(F32)) > 0.0)
    def _():
        tri = (lax.broadcasted_iota(I32, (LANES, LANES), 0)
               < lax.broadcasted_iota(I32, (LANES, LANES), 1)).astype(BF16)
        run = jnp.zeros((nq, 1), F32)
        for j in range(nl // LANES):
            sl = slice(j * LANES, (j + 1) * LANES)
            sj = sc_ref[:, sl]
            eqj = sj == tau
            rank = jnp.dot(eqj.astype(BF16), tri, preferred_element_type=F32) + run
            sel_ref[:, sl] = ((sj > tau) | (eqj & (rank < need))).astype(F32)
            run = run + _row_count(eqj)


def _dsa_core(q, qi, wi, bgate, qpos, kb_ref, vb_ref, kib_ref, sc_ref, sel_ref, topk):
    nq = q.shape[0]
    nl = kb_ref.shape[0]
    kib = kib_ref[...]
    qis = (qi * (IDX_DIM ** -0.5)).astype(BF16)
    wis = wi * (IDX_HEADS ** -0.5)
    scores = jnp.zeros((nq, nl), F32)
    for h in range(IDX_HEADS):
        s = _dot_nt(qis[:, h * IDX_DIM:(h + 1) * IDX_DIM], kib)
        scores = scores + jnp.maximum(s, 0.0) * wis[:, h:h + 1]
    causal = _lane_iota((nq, nl)) <= qpos
    sc_ref[...] = jnp.where(causal, scores, NEG)
    _select_topk(sc_ref, sel_ref, topk)

    sel = (sel_ref[...] > 0.0) & causal
    lane = _lane_iota((nq, LANES))
    group_w = N_HEADS // N_KV_HEADS
    kb = kb_ref[...]
    vb = vb_ref[...]
    qs = q * (HEAD_DIM ** -0.5)
    outs = []
    for h in range(N_HEADS):
        g = h // group_w
        c = (h * HEAD_DIM) // LANES
        x = qs[:, c * LANES:(c + 1) * LANES]
        if (h % 2) != g:
            x = pltpu.roll(x, HEAD_DIM, 1)
        in_g = (lane // HEAD_DIM) == g
        xq = jnp.where(in_g, x, 0.0).astype(BF16)
        s = jnp.where(sel, _dot_nt(xq, kb), NEG)
        mx = jnp.max(s, axis=1, keepdims=True)
        p = jnp.exp(s - mx)
        den = jnp.sum(p, axis=1, keepdims=True)
        o = jnp.dot(p.astype(BF16), vb, preferred_element_type=F32) / den
        o = jnp.where(in_g, o, 0.0)
        if (h % 2) != g:
            o = pltpu.roll(o, HEAD_DIM, 1)
        outs.append(o)
    cols = [outs[2 * c] + outs[2 * c + 1] for c in range(N_HEADS // 2)]
    return jnp.concatenate(cols, axis=1) * _silu(bgate)


def _dsa_prompt_kernel(q_ref, qi_ref, wi_ref, bg_ref, kb_ref, vb_ref, kib_ref, o_ref, sc_ref, sel_ref,
                       *, topk, q_lo):
    qb = pl.program_id(1) + q_lo
    nq = q_ref.shape[1]
    qpos = qb * nq + lax.broadcasted_iota(I32, (nq, 1), 0)
    o_ref[0] = _dsa_core(q_ref[0], qi_ref[0], wi_ref[0], bg_ref[0], qpos,
                         kb_ref.at[0], vb_ref.at[0], kib_ref.at[0], sc_ref, sel_ref, topk)


def _dsa_prompt_bucket(q3, qi3, z3, kb3, vb3, kib3, q_lo, q_hi):
    b, t, _ = q3.shape
    topk = min(TOPK_MAX, t // 4)
    nq = Q_BLOCK
    nl = q_hi * nq
    qi_w = IDX_HEADS * IDX_DIM
    qblk = lambda w: pl.BlockSpec((1, nq, w), lambda bi, i: (bi, i + q_lo, 0))
    zblk = lambda name, w: pl.BlockSpec((1, nq, w), lambda bi, i, c=_col(name, w): (bi, i + q_lo, c))
    keys = lambda w: pl.BlockSpec((1, nl, w), lambda bi, i: (bi, 0, 0))
    return pl.pallas_call(
        functools.partial(_dsa_prompt_kernel, topk=topk, q_lo=q_lo),
        grid=(b, q_hi - q_lo),
        in_specs=[qblk(D_B), qblk(qi_w), zblk("zwi", LANES), zblk("b_gate", D_B),
                  keys(KV_DIM), keys(KV_DIM), keys(IDX_DIM)],
        out_specs=pl.BlockSpec((1, nq, D_B), lambda bi, i: (bi, i, 0)),
        out_shape=jax.ShapeDtypeStruct((b, (q_hi - q_lo) * nq, D_B), F32),
        scratch_shapes=[pltpu.VMEM((nq, nl), F32), pltpu.VMEM((nq, nl), F32)],
        compiler_params=_params(("parallel", "parallel")),
        name="dsa_prompt",
    )(q3, qi3, z3, z3, kb3, vb3, kib3)


DSA_BUCKETS = 8


def _dsa_prompt(q3, qi3, z3, kb3, vb3, kib3):
    nblk = q3.shape[1] // Q_BLOCK
    step = max(1, nblk // DSA_BUCKETS)
    parts = [_dsa_prompt_bucket(q3, qi3, z3, kb3, vb3, kib3, lo, min(lo + step, nblk))
             for lo in range(0, nblk, step)]
    return parts[0] if len(parts) == 1 else jnp.concatenate(parts, axis=1)


def _dsa_sample_kernel(pt_ref, q_ref, qi_ref, wi_ref, bg_ref, knt_ref, vnt_ref, kint_ref,
                       ck_hbm, cv_hbm, cki_hbm, o_ref, kt_ref, vt_ref, kit_ref, sc_ref, sel_ref, sems,
                       *, topk, n_pages, page_base, ts):
    bi = pl.program_id(0)
    nq = q_ref.shape[1]
    past = n_pages * PAGE_SIZE
    nl = kt_ref.shape[1]
    group_w = N_HEADS // N_KV_HEADS

    def copies(p):
        page = pt_ref[bi, p] + page_base
        dst = pl.ds(pl.multiple_of(p * PAGE_SIZE, PAGE_SIZE), PAGE_SIZE)
        return (pltpu.make_async_copy(ck_hbm.at[page], kt_ref.at[:, dst], sems.at[0]),
                pltpu.make_async_copy(cv_hbm.at[page], vt_ref.at[:, dst], sems.at[1]),
                pltpu.make_async_copy(cki_hbm.at[page], kit_ref.at[:, dst], sems.at[2]))

    def start(p, carry):
        for cp in copies(p):
            cp.start()
        return carry

    lax.fori_loop(0, n_pages, start, 0)
    kt_ref[:, past:nl] = knt_ref[0]
    vt_ref[:, past:nl] = vnt_ref[0]
    kit_ref[:, past:nl] = kint_ref[0]

    def wait(p, carry):
        for cp in copies(p):
            cp.wait()
        return carry

    lax.fori_loop(0, n_pages, wait, 0)

    row = lax.broadcasted_iota(I32, (nq, 1), 0)
    qpos = past + jnp.minimum(row, ts - 1)
    stack = lambda x, w, heads: jnp.concatenate([x[:, h * w:(h + 1) * w] for h in heads], axis=0)

    qis = stack(qi_ref[0] * (IDX_DIM ** -0.5), IDX_DIM, range(IDX_HEADS)).astype(BF16)
    s = jnp.dot(qis, kit_ref[...].astype(BF16), preferred_element_type=F32)
    wis = wi_ref[0] * (IDX_HEADS ** -0.5)
    scores = jnp.zeros((nq, nl), F32)
    for h in range(IDX_HEADS):
        scores = scores + jnp.maximum(s[h * nq:(h + 1) * nq], 0.0) * wis[:, h:h + 1]
    causal = _lane_iota((nq, nl)) <= qpos
    sc_ref[...] = jnp.where(causal, scores, NEG)
    _select_topk(sc_ref, sel_ref, topk)

    sel = (sel_ref[...] > 0.0) & causal
    sel_g = jnp.concatenate([sel] * group_w, axis=0)
    qs = q_ref[0] * (HEAD_DIM ** -0.5)
    outs = []
    for g in range(N_KV_HEADS):
        rows = slice(g * HEAD_DIM, (g + 1) * HEAD_DIM)
        qg = stack(qs, HEAD_DIM, range(g * group_w, (g + 1) * group_w)).astype(BF16)
        sg = jnp.dot(qg, kt_ref[rows, :].astype(BF16), preferred_element_type=F32)
        sg = jnp.where(sel_g, sg, NEG)
        mx = jnp.max(sg, axis=1, keepdims=True)
        p = jnp.exp(sg - mx)
        den = jnp.sum(p, axis=1, keepdims=True)
        og = _dot_nt(p.astype(BF16), vt_ref[rows, :].astype(BF16)) / den
        outs += [og[j * nq:(j + 1) * nq] for j in range(group_w)]
    o_ref[0] = jnp.concatenate(outs, axis=1) * _silu(bg_ref[0])


def _dsa_sample(page_table, q8, qi8, wi8, bg8, knt, vnt, kint, ckt, cvt, ckit, layer, depth, ts):
    b, nq, _ = q8.shape
    n_pages = page_table.shape[1]
    past = n_pages * PAGE_SIZE
    nl = past + LANES
    topk = min(TOPK_MAX, (past + ts) // 4)
    n_pool = ckt.shape[0] // depth
    qi_w = IDX_HEADS * IDX_DIM
    blk = lambda r, w: pl.BlockSpec((1, r, w), lambda bi, pt: (bi, 0, 0))
    anyspec = pl.BlockSpec(memory_space=pl.ANY)
    grid_spec = pltpu.PrefetchScalarGridSpec(
        num_scalar_prefetch=1,
        grid=(b,),
        in_specs=[blk(nq, D_B), blk(nq, qi_w), blk(nq, LANES), blk(nq, D_B),
                  blk(KV_DIM, LANES), blk(KV_DIM, LANES), blk(IDX_DIM, LANES),
                  anyspec, anyspec, anyspec],
        out_specs=pl.BlockSpec((1, nq, D_B), lambda bi, pt: (bi, 0, 0)),
        scratch_shapes=[pltpu.VMEM((KV_DIM, nl), F32), pltpu.VMEM((KV_DIM, nl), F32),
                        pltpu.VMEM((IDX_DIM, nl), F32),
                        pltpu.VMEM((nq, nl), F32), pltpu.VMEM((nq, nl), F32),
                        pltpu.SemaphoreType.DMA((3,))])
    return pl.pallas_call(
        functools.partial(_dsa_sample_kernel, topk=topk, n_pages=n_pages, page_base=layer * n_pool, ts=ts),
        grid_spec=grid_spec,
        out_shape=jax.ShapeDtypeStruct((b, nq, D_B), F32),
        compiler_params=_params(("arbitrary",)),
        name="dsa_sample",
    )(page_table, q8, qi8, wi8, bg8, knt, vnt, kint, ckt, cvt, ckit)


def _log_sigmoid(x):
    return -(jnp.maximum(-x, 0.0) + jnp.log1p(jnp.exp(-jnp.abs(x))))


def _hgrn_kernel(cq_ref, cf_ref, ci_ref, cg_ref, lb_ref, ng_ref, s0_ref, yc_ref, s_ref, *, c, valid):
    ci_idx = pl.program_id(1)

    @pl.when(ci_idx == 0)
    def _():
        s_ref[...] = s0_ref[...]

    fx = cf_ref[0]
    lb = lb_ref[...]
    a = jnp.log(jnp.maximum(lb, LB_FLOOR))
    bb = jnp.log1p(-lb) + _log_sigmoid(fx)
    log_f = jnp.maximum(a, bb) + jnp.log1p(jnp.exp(-jnp.abs(a - bb)))
    kk = (1.0 - lb) * _sigmoid(-fx)
    qc = _silu(cq_ref[0])
    iv = ci_ref[0]
    row = lax.broadcasted_iota(I32, (c, 1), 0)
    if valid < c:
        log_f = jnp.where(row < valid, log_f, 0.0)

    rr = lax.broadcasted_iota(I32, (c, c), 0)
    cc = lax.broadcasted_iota(I32, (c, c), 1)
    hs = [slice(h * C_KDIM, (h + 1) * C_KDIM) for h in range(C_HEADS)]
    att = [jnp.zeros((c, c), F32) for _ in range(C_HEADS)]
    cs = log_f
    tot = log_f
    m = 1
    while m < c:
        right = ((row // m) % 2) == 1
        qm = jnp.where(right, qc * jnp.exp(cs), 0.0).astype(BF16)
        km = jnp.where(right, 0.0, kk * jnp.exp(tot - cs)).astype(BF16)
        pair = (rr // (2 * m)) == (cc // (2 * m))
        for h in range(C_HEADS):
            att[h] = att[h] + jnp.where(pair, _dot_nt(qm[:, hs[h]], km[:, hs[h]]), 0.0)
        sib = jnp.where(right, pltpu.roll(tot, m, 0), pltpu.roll(tot, c - m, 0))
        cs = jnp.where(right, cs + sib, cs)
        tot = tot + sib
        m *= 2
    qdec = (qc * jnp.exp(cs)).astype(BF16)
    kdec = (kk * jnp.exp(tot - cs)).astype(BF16)
    ivb = iv.astype(BF16)
    eye_c = rr == cc
    eye_k = (lax.broadcasted_iota(I32, (C_KDIM, C_KDIM), 0)
             == lax.broadcasted_iota(I32, (C_KDIM, C_KDIM), 1))
    ys = []
    for h in range(C_HEADS):
        sl = hs[h]
        diag = jnp.sum(qc[:, sl] * kk[:, sl], axis=1, keepdims=True)
        a_h = att[h] + jnp.where(eye_c, diag, 0.0)
        s_h = s_ref[0, h]
        o = (jnp.dot(a_h.astype(BF16), ivb[:, sl], preferred_element_type=F32)
             + jnp.dot(qdec[:, sl], s_h.astype(BF16), preferred_element_type=F32))
        e_end = jnp.exp(tot[0:1, sl])
        e_col = jnp.sum(jnp.where(eye_k, e_end, 0.0), axis=1, keepdims=True)
        upd = lax.dot_general(kdec[:, sl], ivb[:, sl], (((0,), (0,)), ((), ())),
                              preferred_element_type=F32)
        s_ref[0, h] = e_col * s_h + upd
        ms = jnp.mean(o * o, axis=-1, keepdims=True)
        ys.append(o * lax.rsqrt(ms + EPS) * ng_ref[...])
    yc_ref[0] = jnp.concatenate(ys, axis=1) * _silu(cg_ref[0])


def _hgrn(z3, lb, cng, s0, c, valid):
    b, t, _ = z3.shape
    zblk = lambda name: pl.BlockSpec((1, c, D_C), lambda bi, i, col=_col(name, D_C): (bi, i, col))
    sblk = pl.BlockSpec((1, C_HEADS, C_KDIM, C_VDIM), lambda bi, i: (bi, 0, 0, 0))
    return pl.pallas_call(
        functools.partial(_hgrn_kernel, c=c, valid=valid),
        grid=(b, t // c),
        in_specs=[zblk("cq"), zblk("cf"), zblk("ci"), zblk("c_gate"),
                  pl.BlockSpec((1, C_FDIM), lambda bi, i: (0, 0)),
                  pl.BlockSpec((1, C_VDIM), lambda bi, i: (0, 0)), sblk],
        out_specs=[pl.BlockSpec((1, c, D_C), lambda bi, i: (bi, i, 0)), sblk],
        out_shape=[jax.ShapeDtypeStruct((b, t, D_C), F32),
                   jax.ShapeDtypeStruct((b, C_HEADS, C_KDIM, C_VDIM), F32)],
        compiler_params=_params(("parallel", "arbitrary")),
        name="hgrn",
    )(z3, z3, z3, z3, lb.reshape(1, C_FDIM), cng.reshape(1, C_VDIM), s0)


def _merge_kernel(x_ref, ya_ref, yb_ref, yc_ref, ga_ref, gb_ref, gc_ref, gate_ref,
                  wa_ref, wb_ref, wc_ref, wo_ref, o_ref):
    def proj(y_ref, w_ref):
        return jnp.dot(y_ref[...].astype(BF16), w_ref[...], preferred_element_type=F32)

    m = (_sigmoid(ga_ref[...]) * proj(ya_ref, wa_ref)
         + _sigmoid(gb_ref[...]) * proj(yb_ref, wb_ref)
         + _sigmoid(gc_ref[...]) * proj(yc_ref, wc_ref))
    o_ref[...] = x_ref[...] + gate_ref[0] * jnp.dot(m.astype(BF16), wo_ref[...], preferred_element_type=F32)


def _merge(x2, ya, yb, yc, z, gate3, wa, wb, wc, wo, tm, tiles_per_mod):
    m, d = x2.shape
    r = gate3.shape[1]
    row = lambda w: pl.BlockSpec((tm, w), lambda i: (i, 0))
    zc = lambda name: pl.BlockSpec((tm, d), lambda i, c=_col(name, d): (i, c))
    wspec = lambda k: pl.BlockSpec((k, d), lambda i: (0, 0))
    return pl.pallas_call(
        _merge_kernel,
        grid=(m // tm,),
        in_specs=[row(d), row(D_A), row(D_B), row(D_C), zc("g_a"), zc("g_b"), zc("g_c"),
                  pl.BlockSpec((1, r, d), lambda i: (i // tiles_per_mod, 0, 0)),
                  wspec(D_A), wspec(D_B), wspec(D_C), wspec(d)],
        out_specs=row(d),
        out_shape=jax.ShapeDtypeStruct((m, d), F32),
        compiler_params=_params(("parallel",)),
        name="merge",
    )(x2, ya, yb, yc, z, z, z, gate3, wa, wb, wc, wo)


def _pick_tile(n, pref):
    t = min(pref, n)
    while n % t:
        t //= 2
    return t


def kernel(x_prompt, x_sample, cache_k, cache_v, cache_idx_k, state_conv, state_hgrn, page_table,
           c_prompt, c_sample, w_ada, b_ada, norm_g, w_in, w_dw, b_dw, ln_g, ln_b, q_norm_g, k_norm_g,
           lb_logits, c_norm_g, w_proj_a, w_proj_b, w_proj_c, w_out):
    depth = w_in.shape[0]
    bp, tp, d = x_prompt.shape
    bs, ts, _ = x_sample.shape
    mp, ms = bp * tp, bs * ts
    n_pages = page_table.shape[1]
    past = n_pages * PAGE_SIZE
    assert d == D_MODEL and tp % Q_BLOCK == 0 and tp % CHUNK == 0 and tp >= HALO
    assert ts <= SUBLANES and ms % SUBLANES == 0

    lbp = jax.nn.softmax(lb_logits.astype(F32), axis=0)
    lb_all = jnp.cumsum(lbp, axis=0) - lbp[0:1]
    cos_p, sin_p = _rope_tables(jnp.arange(tp))
    cos_s, sin_s = _rope_tables(past + (jnp.arange(ms) % ts))
    ckt = jnp.transpose(cache_k, (0, 1, 3, 4, 2)).reshape(-1, KV_DIM, PAGE_SIZE)
    cvt = jnp.transpose(cache_v, (0, 1, 3, 4, 2)).reshape(-1, KV_DIM, PAGE_SIZE)
    ckit = jnp.transpose(cache_idx_k, (0, 1, 3, 2)).reshape(-1, IDX_DIM, PAGE_SIZE)
    c_all = jnp.concatenate([c_prompt, c_sample], axis=0)

    tm_p = _pick_tile(mp, 1024)
    tm_p = _pick_tile(tp, tm_p)
    tt = _pick_tile(tp, 256)
    tm_g = _pick_tile(tp, 256)
    zeros_buf = jnp.zeros((bp, HALO, D_A), F32)
    s00 = jnp.zeros((bp, C_HEADS, C_KDIM, C_VDIM), F32)
    pad_rows = lambda a3: jnp.pad(a3, ((0, 0), (0, SUBLANES - ts), (0, 0)))

    xp = x_prompt.reshape(mp, d)
    xs = x_sample.reshape(ms, d)
    outs = [[] for _ in range(10)]
    for l in range(depth):
        mod = _ada(c_all, w_ada[l].astype(BF16), b_ada[l])
        shift, scale, gate = mod[:, :d], mod[:, d:2 * d], mod[:, 2 * d:]
        w_in_p = _pack_w_in(w_in[l]).astype(BF16)
        wa, wb, wc, wo = (w_proj_a[l].astype(BF16), w_proj_b[l].astype(BF16),
                          w_proj_c[l].astype(BF16), w_out[l].astype(BF16))
        qg = jnp.tile(q_norm_g[l], LANES // HEAD_DIM).reshape(1, LANES)
        kg = jnp.tile(k_norm_g[l], LANES // HEAD_DIM).reshape(1, LANES)
        lb = lb_all[l]

        per_seq = lambda a: a[:bp].reshape(bp, 1, d)
        h = _prenorm(xp, norm_g[l], per_seq(scale), per_seq(shift), tm_p, tp // tm_p)
        z = _matmul(h, w_in_p, tm_p, NZ // 6)
        z3 = z.reshape(bp, tp, NZ)
        ya, utail = _conv_prompt(z3, zeros_buf, w_dw[l], b_dw[l], ln_g[l], ln_b[l], tt)
        q, k, qi, ki, v, kb, vb, kib = _qkrope(z, cos_p, sin_p, qg, kg, tm_p, tp // tm_p)
        r3 = lambda a: a.reshape(bp, tp, a.shape[-1])
        yb = _dsa_prompt(r3(q), r3(qi), z3, r3(kb), r3(vb), r3(kib))
        yc, s_p = _hgrn(z3, lb, c_norm_g[l], s00, CHUNK, CHUNK)
        xp = _merge(xp, ya.reshape(mp, D_A), yb.reshape(mp, D_B), yc.reshape(mp, D_C), z,
                    per_seq(gate), wa, wb, wc, wo, tm_g, tp // tm_g)
        outs[0].append(k.reshape(bp, tp, N_KV_HEADS, HEAD_DIM))
        outs[1].append(v.reshape(bp, tp, N_KV_HEADS, HEAD_DIM))
        outs[2].append(ki.reshape(bp, tp, IDX_DIM))
        outs[3].append(utail[:, HALO - (CONV_W - 1):, :])
        outs[4].append(s_p)

        per_row = lambda a: jnp.repeat(a[bp:], ts, axis=0).reshape(1, ms, d)
        hs = _prenorm(xs, norm_g[l], per_row(scale), per_row(shift), ms, 1)
        zs = _matmul(hs, w_in_p, ms, NZ // 6)
        zs3 = zs.reshape(bs, ts, NZ)
        seg_t = lambda name: jnp.swapaxes(
            zs3[:, :, _PACK_OFF[name]:_PACK_OFF[name] + D_A], 0, 1)
        ya_t, u_t = _conv_sample(seg_t("a_val"), seg_t("a_glu"), seg_t("a_gate"),
                                 jnp.swapaxes(state_conv[l], 0, 1), w_dw[l], b_dw[l], ln_g[l], ln_b[l])
        ya_s = jnp.swapaxes(ya_t, 0, 1).reshape(ms, D_A)
        q, k, qi, ki, v, _, _, _ = _qkrope(zs, cos_s, sin_s, qg, kg, ms, 1)
        p8 = lambda a: pad_rows(a.reshape(bs, ts, a.shape[-1]))
        zs8 = pad_rows(zs3)
        seg8 = lambda name, w: zs8[:, :, _PACK_OFF[name]:_PACK_OFF[name] + w]
        new_t = lambda a: jnp.pad(jnp.swapaxes(a.reshape(bs, ts, a.shape[-1]), 1, 2),
                                  ((0, 0), (0, 0), (0, LANES - ts)))
        yb8 = _dsa_sample(page_table, p8(q), p8(qi), seg8("zwi", LANES), seg8("b_gate", D_B),
                          new_t(k), new_t(v), new_t(ki), ckt, cvt, ckit, l, depth, ts)
        yc8, s_s = _hgrn(zs8, lb, c_norm_g[l], state_hgrn[l], SUBLANES, ts)
        xs = _merge(xs, ya_s, yb8[:, :ts].reshape(ms, D_B), yc8[:, :ts].reshape(ms, D_C), zs,
                    per_row(gate), wa, wb, wc, wo, ms, 1)
        outs[5].append(k.reshape(bs, ts, N_KV_HEADS, HEAD_DIM))
        outs[6].append(v.reshape(bs, ts, N_KV_HEADS, HEAD_DIM))
        outs[7].append(ki.reshape(bs, ts, IDX_DIM))
        outs[8].append(jnp.concatenate([state_conv[l][:, ts:], jnp.swapaxes(u_t, 0, 1)], axis=1))
        outs[9].append(s_s)

    st = [jnp.stack(o) for o in outs]
    return (xp.reshape(bp, tp, d), xs.reshape(bs, ts, d), st[0], st[1], st[2], st[3], st[4],
            st[5], st[6], st[7], st[8], st[9])
```

```python
import functools
import math

import jax
import jax.numpy as jnp
import numpy as np
from jax import lax
from jax.experimental import pallas as pl
from jax.experimental.pallas import tpu as pltpu

F32 = jnp.float32
BF16 = jnp.bfloat16
I32 = jnp.int32

LANES = 128
SUBLANES = 8
VMEM_LIMIT = 56 * 1024 * 1024

D_MODEL = 1024
EPS = 1e-6
NEG = -1e30
LB_FLOOR = 1e-30
D_A = D_MODEL // 2
CONV_W = 31
HALO = 32
N_HEADS = 8
N_KV_HEADS = 2
HEAD_DIM = 64
D_B = N_HEADS * HEAD_DIM
KV_DIM = N_KV_HEADS * HEAD_DIM
IDX_HEADS = 4
IDX_DIM = 64
TOPK_MAX = 256
ROPE_THETA = 10000.0
Q_BLOCK = 128
C_HEADS = 4
C_KDIM = 128
C_VDIM = D_MODEL // 2 // C_HEADS
D_C = C_HEADS * C_VDIM
C_FDIM = C_HEADS * C_KDIM
CHUNK = 64
PAGE_SIZE = 128

_IN_NAMES = ("a_val", "a_glu", "a_gate", "zq", "zk", "zv", "zqi", "zki", "zwi", "b_gate",
             "cq", "cf", "ci", "c_gate", "g_a", "g_b", "g_c")
_IN_WIDTHS = (D_A, D_A, D_A, D_B, KV_DIM, KV_DIM, IDX_HEADS * IDX_DIM, IDX_DIM, IDX_HEADS, D_B,
              C_FDIM, C_FDIM, D_C, D_C, D_MODEL, D_MODEL, D_MODEL)
_IN_OFFS = dict(zip(_IN_NAMES, np.concatenate([[0], np.cumsum(_IN_WIDTHS)[:-1]]).tolist()))
_IN_W = dict(zip(_IN_NAMES, _IN_WIDTHS))
_PACK_ORDER = ("g_a", "g_b", "g_c", "a_val", "a_glu", "a_gate", "zq", "b_gate", "cq", "cf", "ci",
               "c_gate", "zqi", "zk", "zv", "zki", "zwi")


def _round_up(n, m):
    return (n + m - 1) // m * m


_PACK_W = {n: _round_up(_IN_W[n], LANES) for n in _PACK_ORDER}
_PACK_OFF = {}
_o = 0
for _n in _PACK_ORDER:
    assert _o % _PACK_W[_n] == 0
    _PACK_OFF[_n] = _o
    _o += _PACK_W[_n]
NZ = _o


def _pack_w_in_t(w):
    wt = jnp.swapaxes(w, 0, 1)
    parts = []
    for n in _PACK_ORDER:
        seg = wt[_IN_OFFS[n]:_IN_OFFS[n] + _IN_W[n]]
        pad = _PACK_W[n] - _IN_W[n]
        if pad:
            seg = jnp.pad(seg, ((0, pad), (0, 0)))
        parts.append(seg)
    return jnp.concatenate(parts, axis=0)


def _col(name, width):
    assert _PACK_OFF[name] % width == 0
    return _PACK_OFF[name] // width


def _sigmoid(x):
    return jax.nn.sigmoid(x)


def _silu(x):
    return x * jax.nn.sigmoid(x)


def _params(sem):
    return pltpu.CompilerParams(dimension_semantics=sem, vmem_limit_bytes=VMEM_LIMIT)


def _ada_kernel(c_ref, w_ref, b_ref, o_ref):
    c = c_ref[...]
    o_ref[...] = jnp.dot(_silu(c).astype(BF16), w_ref[...], preferred_element_type=F32) + b_ref[...]


def _ada(c, w_bf, b):
    n, d = c.shape
    nout = w_bf.shape[1]
    tn = D_MODEL
    return pl.pallas_call(
        _ada_kernel,
        grid=(nout // tn,),
        in_specs=[pl.BlockSpec((n, d), lambda j: (0, 0)),
                  pl.BlockSpec((d, tn), lambda j: (0, j)),
                  pl.BlockSpec((1, tn), lambda j: (0, j))],
        out_specs=pl.BlockSpec((n, tn), lambda j: (0, j)),
        out_shape=jax.ShapeDtypeStruct((n, nout), F32),
        compiler_params=_params(("arbitrary",)),
        name="ada",
    )(c, w_bf, b.reshape(1, nout))


def _prenorm_kernel(x_ref, g_ref, sc_ref, sh_ref, o_ref):
    x = x_ref[...]
    ms = jnp.mean(x * x, axis=-1, keepdims=True)
    y = x * lax.rsqrt(ms + EPS) * g_ref[...]
    o_ref[...] = (y * (1.0 + sc_ref[0]) + sh_ref[0]).astype(o_ref.dtype)


def _prenorm(x2, g, scale3, shift3, tm, tiles_per_mod):
    m, d = x2.shape
    r = scale3.shape[1]
    return pl.pallas_call(
        _prenorm_kernel,
        grid=(m // tm,),
        in_specs=[pl.BlockSpec((tm, d), lambda i: (i, 0)),
                  pl.BlockSpec((1, d), lambda i: (0, 0)),
                  pl.BlockSpec((1, r, d), lambda i: (i // tiles_per_mod, 0, 0)),
                  pl.BlockSpec((1, r, d), lambda i: (i // tiles_per_mod, 0, 0))],
        out_specs=pl.BlockSpec((tm, d), lambda i: (i, 0)),
        out_shape=jax.ShapeDtypeStruct((m, d), BF16),
        compiler_params=_params(("parallel",)),
        name="prenorm",
    )(x2, g.reshape(1, d), scale3, shift3)


def _mm_kernel(a_ref, bt_ref, o_ref):
    o_ref[...] = lax.dot_general(a_ref[...], bt_ref[...], (((1,), (1,)), ((), ())),
                                 preferred_element_type=F32)


def _matmul_nt(a_bf, bt_bf, tm, tn):
    m, k = a_bf.shape
    n = bt_bf.shape[0]
    return pl.pallas_call(
        _mm_kernel,
        grid=(n // tn, m // tm),
        in_specs=[pl.BlockSpec((tm, k), lambda j, i: (i, 0)),
                  pl.BlockSpec((tn, k), lambda j, i: (j, 0))],
        out_specs=pl.BlockSpec((tm, tn), lambda j, i: (i, j)),
        out_shape=jax.ShapeDtypeStruct((m, n), F32),
        compiler_params=_params(("parallel", "parallel")),
        name="inproj",
    )(a_bf, bt_bf)


def _ln_swish_gate(y, gate, bdw, lng, lnb):
    y = y + bdw
    mu = jnp.mean(y, axis=-1, keepdims=True)
    yc = y - mu
    var = jnp.mean(yc * yc, axis=-1, keepdims=True)
    yn = yc * lax.rsqrt(var + EPS) * lng + lnb
    return _silu(yn) * _silu(gate)


def _conv_prompt_kernel(val_ref, glu_ref, gate_ref, valh_ref, gluh_ref, buf_ref, w_ref, bdw_ref,
                        lng_ref, lnb_ref, ya_ref, utail_ref, f_ref, g_ref, *, tt):
    ti = pl.program_id(1)
    u = val_ref[0] * _sigmoid(glu_ref[0])
    uh = valh_ref[0] * _sigmoid(gluh_ref[0])
    f_ref[0:HALO, :] = jnp.where(ti == 0, buf_ref[0], uh)
    f_ref[HALO:HALO + tt, :] = u
    nrows = g_ref.shape[1]
    for p in range(1, SUBLANES):
        g_ref[p - 1] = f_ref[pl.ds(p, nrows), :]
    off = HALO - (CONV_W - 1)
    acc = jnp.zeros((tt, D_A), F32)
    for j in range(CONV_W):
        p = (off + j) % SUBLANES
        a = off + j - p
        win = f_ref[pl.ds(a, tt), :] if p == 0 else g_ref[p - 1, pl.ds(a, tt), :]
        acc = acc + w_ref[j:j + 1, :] * win
    ya_ref[0] = _ln_swish_gate(acc, gate_ref[0], bdw_ref[...], lng_ref[...], lnb_ref[...])

    @pl.when(ti == pl.num_programs(1) - 1)
    def _():
        utail_ref[0] = f_ref[tt:tt + HALO, :]


def _conv_prompt(z3, buf32, w_dw, b_dw, ln_g, ln_b, tt):
    b, t, _ = z3.shape
    hb = tt // HALO
    cur = lambda name: pl.BlockSpec((1, tt, D_A), lambda bi, ti, c=_col(name, D_A): (bi, ti, c))
    halo = lambda name: pl.BlockSpec(
        (1, HALO, D_A), lambda bi, ti, c=_col(name, D_A): (bi, jnp.maximum(ti * hb - 1, 0), c))
    vec = pl.BlockSpec((1, D_A), lambda bi, ti: (0, 0))
    wpad = jnp.pad(w_dw, ((0, HALO - CONV_W), (0, 0)))
    return pl.pallas_call(
        functools.partial(_conv_prompt_kernel, tt=tt),
        grid=(b, t // tt),
        in_specs=[cur("a_val"), cur("a_glu"), cur("a_gate"), halo("a_val"), halo("a_glu"),
                  pl.BlockSpec((1, HALO, D_A), lambda bi, ti: (bi, 0, 0)),
                  pl.BlockSpec((HALO, D_A), lambda bi, ti: (0, 0)), vec, vec, vec],
        out_specs=[pl.BlockSpec((1, tt, D_A), lambda bi, ti: (bi, ti, 0)),
                   pl.BlockSpec((1, HALO, D_A), lambda bi, ti: (bi, 0, 0))],
        out_shape=[jax.ShapeDtypeStruct((b, t, D_A), F32),
                   jax.ShapeDtypeStruct((b, HALO, D_A), F32)],
        scratch_shapes=[pltpu.VMEM((HALO + tt, D_A), F32),
                        pltpu.VMEM((SUBLANES - 1, HALO + tt - SUBLANES, D_A), F32)],
        compiler_params=_params(("parallel", "arbitrary")),
        name="conv_prompt",
    )(z3, z3, z3, z3, z3, buf32, wpad, b_dw.reshape(1, D_A), ln_g.reshape(1, D_A), ln_b.reshape(1, D_A))


def _conv_sample_kernel(val_ref, glu_ref, gate_ref, buf_ref, w_ref, bdw_ref, lng_ref, lnb_ref,
                        ya_ref, u_ref, *, ts):
    nb = CONV_W - 1
    for t in range(ts):
        u_ref[t] = val_ref[t] * _sigmoid(glu_ref[t])
    for t in range(ts):
        acc = jnp.zeros(u_ref.shape[1:], F32)
        for j in range(CONV_W):
            r = t + j
            src = buf_ref[r] if r < nb else u_ref[r - nb]
            acc = acc + w_ref[j:j + 1, :] * src
        ya_ref[t] = _ln_swish_gate(acc, gate_ref[t], bdw_ref[...], lng_ref[...], lnb_ref[...])


def _conv_sample(val_t, glu_t, gate_t, buf_t, w_dw, b_dw, ln_g, ln_b):
    ts, b, _ = val_t.shape
    assert ts <= CONV_W - 1
    return pl.pallas_call(
        functools.partial(_conv_sample_kernel, ts=ts),
        out_shape=[jax.ShapeDtypeStruct((ts, b, D_A), F32), jax.ShapeDtypeStruct((ts, b, D_A), F32)],
        compiler_params=pltpu.CompilerParams(vmem_limit_bytes=VMEM_LIMIT),
        name="conv_sample",
    )(val_t, glu_t, gate_t, buf_t, w_dw, b_dw.reshape(1, D_A), ln_g.reshape(1, D_A), ln_b.reshape(1, D_A))


def _lane_iota(shape):
    return lax.broadcasted_iota(I32, shape, len(shape) - 1)


def _swap_half(x):
    w = x.shape[-1]
    half = HEAD_DIM // 2
    first = (_lane_iota(x.shape) % HEAD_DIM) < half
    return jnp.where(first, pltpu.roll(x, w - half, 1), pltpu.roll(x, half, 1))


def _group_sum(s):
    same_group = (lax.broadcasted_iota(I32, (LANES, LANES), 0) // HEAD_DIM
                  == lax.broadcasted_iota(I32, (LANES, LANES), 1) // HEAD_DIM).astype(BF16)
    hi = s.astype(BF16)
    r1 = s - hi.astype(F32)
    mid = r1.astype(BF16)
    lo = (r1 - mid.astype(F32)).astype(BF16)
    cols = []
    for c in range(s.shape[-1] // LANES):
        sl = slice(c * LANES, (c + 1) * LANES)
        cols.append(jnp.dot(hi[:, sl], same_group, preferred_element_type=F32)
                    + jnp.dot(mid[:, sl], same_group, preferred_element_type=F32)
                    + jnp.dot(lo[:, sl], same_group, preferred_element_type=F32))
    return cols[0] if len(cols) == 1 else jnp.concatenate(cols, axis=-1)


def _tile_lanes(t, w):
    reps = w // t.shape[-1]
    return t if reps == 1 else jnp.concatenate([t] * reps, axis=-1)


def _rope(x, cos, sin):
    w = x.shape[-1]
    return x * _tile_lanes(cos, w) + _swap_half(x) * _tile_lanes(sin, w)


def _head_rms(x, g):
    ms = _group_sum(x * x) * (1.0 / HEAD_DIM)
    y = x * lax.rsqrt(ms + EPS)
    return y if g is None else y * _tile_lanes(g, x.shape[-1])


def _qkrope_kernel(zq_ref, zk_ref, zqi_ref, zki_ref, zv_ref, cos_ref, sin_ref, qg_ref, kg_ref,
                   q_ref, k_ref, qi_ref, ki_ref, v_ref, kb_ref, vb_ref, kib_ref):
    cos = cos_ref[...]
    sin = sin_ref[...]
    q_ref[...] = _rope(_head_rms(zq_ref[...], qg_ref[...]), cos, sin)
    k = _rope(_head_rms(zk_ref[...], kg_ref[...]), cos, sin)
    k_ref[...] = k
    kb_ref[...] = k.astype(BF16)
    qi_ref[...] = _rope(zqi_ref[...], cos, sin)
    ki = _rope(_head_rms(zki_ref[...], None), cos, sin)[:, :IDX_DIM]
    ki_ref[...] = ki
    kib_ref[...] = ki.astype(BF16)
    v = zv_ref[...]
    v_ref[...] = v
    vb_ref[...] = v.astype(BF16)


def _qkrope(z, cos_t, sin_t, qg, kg, tm, table_tiles):
    m = z.shape[0]
    zc = lambda name, w: pl.BlockSpec((tm, w), lambda i, c=_col(name, w): (i, c))
    tab = pl.BlockSpec((tm, LANES), lambda i: (i % table_tiles, 0))
    vec = pl.BlockSpec((1, LANES), lambda i: (0, 0))
    row = lambda w: pl.BlockSpec((tm, w), lambda i: (i, 0))
    qi_w = IDX_HEADS * IDX_DIM
    return pl.pallas_call(
        _qkrope_kernel,
        grid=(m // tm,),
        in_specs=[zc("zq", D_B), zc("zk", KV_DIM), zc("zqi", qi_w), zc("zki", LANES), zc("zv", KV_DIM),
                  tab, tab, vec, vec],
        out_specs=[row(D_B), row(KV_DIM), row(qi_w), row(IDX_DIM), row(KV_DIM),
                   row(KV_DIM), row(KV_DIM), row(IDX_DIM)],
        out_shape=[jax.ShapeDtypeStruct((m, D_B), F32), jax.ShapeDtypeStruct((m, KV_DIM), F32),
                   jax.ShapeDtypeStruct((m, qi_w), F32), jax.ShapeDtypeStruct((m, IDX_DIM), F32),
                   jax.ShapeDtypeStruct((m, KV_DIM), F32),
                   jax.ShapeDtypeStruct((m, KV_DIM), BF16), jax.ShapeDtypeStruct((m, KV_DIM), BF16),
                   jax.ShapeDtypeStruct((m, IDX_DIM), BF16)],
        compiler_params=_params(("parallel",)),
        name="qkrope",
    )(z, z, z, z, z, cos_t, sin_t, qg, kg)


def _rope_tables(pos):
    half = HEAD_DIM // 2
    inv = ROPE_THETA ** (-jnp.arange(half, dtype=F32) / half)
    ang = pos.astype(F32)[:, None] * inv[None, :]
    cos = jnp.cos(ang)
    sin = jnp.sin(ang)
    cos64 = jnp.concatenate([cos, cos], axis=1)
    sin64 = jnp.concatenate([-sin, sin], axis=1)
    reps = LANES // HEAD_DIM
    return jnp.tile(cos64, (1, reps)), jnp.tile(sin64, (1, reps))


_SIGN = np.int32(-2 ** 31)
_MAG = np.int32(0x7FFFFFFF)
SEARCH_UNROLL = 4
SAMPLE_DIGIT_BITS = 3


def _dot_nt(a, b):
    return lax.dot_general(a, b, (((1,), (1,)), ((), ())), preferred_element_type=F32)


def _row_count(mask):
    return jnp.sum(mask.astype(F32), axis=1, keepdims=True)


def _ordered_to_float(t):
    key = t ^ _SIGN
    return lax.bitcast_convert_type(jnp.where(key < 0, key ^ _MAG, key), F32)


def _threshold_bits(sc_ref, kf):
    nq, nl = sc_ref.shape

    def cond(c):
        i, _, cnt_t = c
        return (i < 32) & (jnp.max(jnp.abs(cnt_t - kf)) > 0.0)

    def body(c):
        i, t, cnt_t = c
        for j in range(SEARCH_UNROLL):
            cand = t | jnp.left_shift(jnp.int32(1), 31 - (i + j))
            cnt = _row_count(sc_ref[...] >= _ordered_to_float(cand))
            ok = cnt >= kf
            t = jnp.where(ok, cand, t)
            cnt_t = jnp.where(ok, cnt, cnt_t)
        return i + SEARCH_UNROLL, t, cnt_t

    init = (jnp.int32(0), jnp.zeros((nq, 1), I32), jnp.full((nq, 1), float(nl), F32))
    return lax.while_loop(cond, body, init)[1]


def _threshold_digits(sc_ref, kf, digit_bits):
    nq, _ = sc_ref.shape
    t = jnp.zeros((nq, 1), I32)
    pos = 32
    while pos > 0:
        nb = (pos % digit_bits) or digit_bits
        pos -= nb
        digit = jnp.zeros((nq, 1), I32)
        for j in range(1, 2 ** nb):
            cand = t | np.uint32(j << pos).astype(np.int32)
            ok = _row_count(sc_ref[...] >= _ordered_to_float(cand)) >= kf
            digit = digit + ok.astype(I32)
        t = t | jnp.left_shift(digit, pos)
    return t


def _select_topk(sc_ref, sel_ref, topk, digit_bits=1):
    nq, nl = sc_ref.shape
    if nl == topk:
        sel_ref[...] = jnp.ones((nq, nl), F32)
        return
    kf = float(topk)
    t = _threshold_bits(sc_ref, kf) if digit_bits == 1 else _threshold_digits(sc_ref, kf, digit_bits)
    tau = jnp.where(t == 0, -jnp.inf, _ordered_to_float(t))
    sc = sc_ref[...]
    gt = sc > tau
    eq = sc == tau
    need = kf - _row_count(gt)
    sel_ref[...] = (gt | eq).astype(F32)
    tie_rows = (_row_count(eq) > need) & (tau > NEG)

    @pl.when(jnp.max(tie_rows.astype(F32)) > 0.0)
    def _():
        tri = (lax.broadcasted_iota(I32, (LANES, LANES), 0)
               < lax.broadcasted_iota(I32, (LANES, LANES), 1)).astype(BF16)
        run = jnp.zeros((nq, 1), F32)
        for j in range(nl // LANES):
            sl = slice(j * LANES, (j + 1) * LANES)
            sj = sc_ref[:, sl]
            eqj = sj == tau
            rank = jnp.dot(eqj.astype(BF16), tri, preferred_element_type=F32) + run
            sel_ref[:, sl] = ((sj > tau) | (eqj & (rank < need))).astype(F32)
            run = run + _row_count(eqj)


def _dsa_core(q, qi, wi, bgate, qpos, kb_ref, vb_ref, kib_ref, sc_ref, sel_ref, topk):
    nq = q.shape[0]
    nl = kb_ref.shape[0]
    kib = kib_ref[...]
    qis = (qi * (IDX_DIM ** -0.5)).astype(BF16)
    wis = wi * (IDX_HEADS ** -0.5)
    scores = jnp.zeros((nq, nl), F32)
    for h in range(IDX_HEADS):
        s = _dot_nt(qis[:, h * IDX_DIM:(h + 1) * IDX_DIM], kib)
        scores = scores + jnp.maximum(s, 0.0) * wis[:, h:h + 1]
    causal = _lane_iota((nq, nl)) <= qpos
    sc_ref[...] = jnp.where(causal, scores, NEG)
    _select_topk(sc_ref, sel_ref, topk)

    sel = (sel_ref[...] > 0.0) & causal
    lane = _lane_iota((nq, LANES))
    group_w = N_HEADS // N_KV_HEADS
    kb = kb_ref[...]
    vb = vb_ref[...]
    qs = q * (HEAD_DIM ** -0.5)
    outs = []
    for h in range(N_HEADS):
        g = h // group_w
        c = (h * HEAD_DIM) // LANES
        x = qs[:, c * LANES:(c + 1) * LANES]
        if (h % 2) != g:
            x = pltpu.roll(x, HEAD_DIM, 1)
        in_g = (lane // HEAD_DIM) == g
        xq = jnp.where(in_g, x, 0.0).astype(BF16)
        s = jnp.where(sel, _dot_nt(xq, kb), NEG)
        mx = jnp.max(s, axis=1, keepdims=True)
        p = jnp.exp(s - mx)
        den = jnp.sum(p, axis=1, keepdims=True)
        o = jnp.dot(p.astype(BF16), vb, preferred_element_type=F32) / den
        o = jnp.where(in_g, o, 0.0)
        if (h % 2) != g:
            o = pltpu.roll(o, HEAD_DIM, 1)
        outs.append(o)
    cols = [outs[2 * c] + outs[2 * c + 1] for c in range(N_HEADS // 2)]
    return jnp.concatenate(cols, axis=1) * _silu(bgate)


def _dsa_prompt_kernel(q_ref, qi_ref, wi_ref, bg_ref, kb_ref, vb_ref, kib_ref, o_ref, sc_ref, sel_ref,
                       *, topk, q_lo):
    qb = pl.program_id(1) + q_lo
    nq = q_ref.shape[1]
    qpos = qb * nq + lax.broadcasted_iota(I32, (nq, 1), 0)
    o_ref[0] = _dsa_core(q_ref[0], qi_ref[0], wi_ref[0], bg_ref[0], qpos,
                         kb_ref.at[0], vb_ref.at[0], kib_ref.at[0], sc_ref, sel_ref, topk)


def _dsa_prompt_bucket(q3, qi3, z3, kb3, vb3, kib3, q_lo, q_hi):
    b, t, _ = q3.shape
    topk = min(TOPK_MAX, t // 4)
    nq = Q_BLOCK
    nl = q_hi * nq
    qi_w = IDX_HEADS * IDX_DIM
    qblk = lambda w: pl.BlockSpec((1, nq, w), lambda bi, i: (bi, i + q_lo, 0))
    zblk = lambda name, w: pl.BlockSpec((1, nq, w), lambda bi, i, c=_col(name, w): (bi, i + q_lo, c))
    keys = lambda w: pl.BlockSpec((1, nl, w), lambda bi, i: (bi, 0, 0))
    return pl.pallas_call(
        functools.partial(_dsa_prompt_kernel, topk=topk, q_lo=q_lo),
        grid=(b, q_hi - q_lo),
        in_specs=[qblk(D_B), qblk(qi_w), zblk("zwi", LANES), zblk("b_gate", D_B),
                  keys(KV_DIM), keys(KV_DIM), keys(IDX_DIM)],
        out_specs=pl.BlockSpec((1, nq, D_B), lambda bi, i: (bi, i, 0)),
        out_shape=jax.ShapeDtypeStruct((b, (q_hi - q_lo) * nq, D_B), F32),
        scratch_shapes=[pltpu.VMEM((nq, nl), F32), pltpu.VMEM((nq, nl), F32)],
        compiler_params=_params(("parallel", "parallel")),
        name="dsa_prompt",
    )(q3, qi3, z3, z3, kb3, vb3, kib3)


DSA_BUCKETS = 8


def _dsa_prompt(q3, qi3, z3, kb3, vb3, kib3):
    nblk = q3.shape[1] // Q_BLOCK
    step = max(1, nblk // DSA_BUCKETS)
    parts = [_dsa_prompt_bucket(q3, qi3, z3, kb3, vb3, kib3, lo, min(lo + step, nblk))
             for lo in range(0, nblk, step)]
    return parts[0] if len(parts) == 1 else jnp.concatenate(parts, axis=1)


def _dsa_sample_kernel(pt_ref, q_ref, qi_ref, wi_ref, bg_ref, knt_ref, vnt_ref, kint_ref,
                       ck_hbm, cv_hbm, cki_hbm, o_ref, kt_ref, vt_ref, kit_ref, sc_ref, sel_ref, sems,
                       *, topk, n_pages, page_base, ts):
    bi = pl.program_id(0)
    nb = pl.num_programs(0)
    nq = q_ref.shape[1]
    past = n_pages * PAGE_SIZE
    nl = kt_ref.shape[2]
    group_w = N_HEADS // N_KV_HEADS
    slot = bi % 2

    def copies(seq, s, p):
        page = pt_ref[seq, p] + page_base
        dst = pl.ds(pl.multiple_of(p * PAGE_SIZE, PAGE_SIZE), PAGE_SIZE)
        return (pltpu.make_async_copy(ck_hbm.at[page], kt_ref.at[s, :, dst], sems.at[s, 0]),
                pltpu.make_async_copy(cv_hbm.at[page], vt_ref.at[s, :, dst], sems.at[s, 1]),
                pltpu.make_async_copy(cki_hbm.at[page], kit_ref.at[s, :, dst], sems.at[s, 2]))

    def start_gather(seq, s):
        def start(p, carry):
            for cp in copies(seq, s, p):
                cp.start()
            return carry
        lax.fori_loop(0, n_pages, start, 0)

    @pl.when(bi == 0)
    def _():
        start_gather(0, 0)

    @pl.when(bi + 1 < nb)
    def _():
        start_gather(bi + 1, 1 - slot)

    kt_ref[slot, :, past:nl] = knt_ref[0]
    vt_ref[slot, :, past:nl] = vnt_ref[0]
    kit_ref[slot, :, past:nl] = kint_ref[0]

    def wait(p, carry):
        for cp in copies(bi, slot, p):
            cp.wait()
        return carry

    lax.fori_loop(0, n_pages, wait, 0)
    kt_ref, vt_ref, kit_ref = kt_ref.at[slot], vt_ref.at[slot], kit_ref.at[slot]

    row = lax.broadcasted_iota(I32, (nq, 1), 0)
    qpos = past + jnp.minimum(row, ts - 1)
    stack = lambda x, w, heads: jnp.concatenate([x[:, h * w:(h + 1) * w] for h in heads], axis=0)

    qis = stack(qi_ref[0] * (IDX_DIM ** -0.5), IDX_DIM, range(IDX_HEADS)).astype(BF16)
    s = jnp.dot(qis, kit_ref[...].astype(BF16), preferred_element_type=F32)
    wis = wi_ref[0] * (IDX_HEADS ** -0.5)
    scores = jnp.zeros((nq, nl), F32)
    for h in range(IDX_HEADS):
        scores = scores + jnp.maximum(s[h * nq:(h + 1) * nq], 0.0) * wis[:, h:h + 1]
    causal = _lane_iota((nq, nl)) <= qpos
    sc_ref[...] = jnp.where(causal, scores, NEG)
    _select_topk(sc_ref, sel_ref, topk, digit_bits=SAMPLE_DIGIT_BITS)

    sel = (sel_ref[...] > 0.0) & causal
    sel_g = jnp.concatenate([sel] * group_w, axis=0)
    qs = q_ref[0] * (HEAD_DIM ** -0.5)
    outs = []
    for g in range(N_KV_HEADS):
        rows = slice(g * HEAD_DIM, (g + 1) * HEAD_DIM)
        qg = stack(qs, HEAD_DIM, range(g * group_w, (g + 1) * group_w)).astype(BF16)
        sg = jnp.dot(qg, kt_ref[rows, :].astype(BF16), preferred_element_type=F32)
        sg = jnp.where(sel_g, sg, NEG)
        mx = jnp.max(sg, axis=1, keepdims=True)
        p = jnp.exp(sg - mx)
        den = jnp.sum(p, axis=1, keepdims=True)
        og = _dot_nt(p.astype(BF16), vt_ref[rows, :].astype(BF16)) / den
        outs += [og[j * nq:(j + 1) * nq] for j in range(group_w)]
    o_ref[0] = jnp.concatenate(outs, axis=1) * _silu(bg_ref[0])


def _dsa_sample(page_table, q8, qi8, wi8, bg8, knt, vnt, kint, ckt, cvt, ckit, layer, depth, ts):
    b, nq, _ = q8.shape
    n_pages = page_table.shape[1]
    past = n_pages * PAGE_SIZE
    nl = past + LANES
    topk = min(TOPK_MAX, (past + ts) // 4)
    n_pool = ckt.shape[0] // depth
    qi_w = IDX_HEADS * IDX_DIM
    blk = lambda r, w: pl.BlockSpec((1, r, w), lambda bi, pt: (bi, 0, 0))
    anyspec = pl.BlockSpec(memory_space=pl.ANY)
    grid_spec = pltpu.PrefetchScalarGridSpec(
        num_scalar_prefetch=1,
        grid=(b,),
        in_specs=[blk(nq, D_B), blk(nq, qi_w), blk(nq, LANES), blk(nq, D_B),
                  blk(KV_DIM, LANES), blk(KV_DIM, LANES), blk(IDX_DIM, LANES),
                  anyspec, anyspec, anyspec],
        out_specs=pl.BlockSpec((1, nq, D_B), lambda bi, pt: (bi, 0, 0)),
        scratch_shapes=[pltpu.VMEM((2, KV_DIM, nl), F32), pltpu.VMEM((2, KV_DIM, nl), F32),
                        pltpu.VMEM((2, IDX_DIM, nl), F32),
                        pltpu.VMEM((nq, nl), F32), pltpu.VMEM((nq, nl), F32),
                        pltpu.SemaphoreType.DMA((2, 3))])
    return pl.pallas_call(
        functools.partial(_dsa_sample_kernel, topk=topk, n_pages=n_pages, page_base=layer * n_pool, ts=ts),
        grid_spec=grid_spec,
        out_shape=jax.ShapeDtypeStruct((b, nq, D_B), F32),
        compiler_params=_params(("arbitrary",)),
        name="dsa_sample",
    )(page_table, q8, qi8, wi8, bg8, knt, vnt, kint, ckt, cvt, ckit)


def _log_sigmoid(x):
    return -(jnp.maximum(-x, 0.0) + jnp.log1p(jnp.exp(-jnp.abs(x))))


def _hgrn_kernel(cq_ref, cf_ref, ci_ref, cg_ref, lb_ref, ng_ref, s0_ref, yc_ref, s_ref, *, c, valid):
    ci_idx = pl.program_id(1)

    @pl.when(ci_idx == 0)
    def _():
        s_ref[...] = s0_ref[...]

    fx = cf_ref[0]
    lb = lb_ref[...]
    a = jnp.log(jnp.maximum(lb, LB_FLOOR))
    bb = jnp.log1p(-lb) + _log_sigmoid(fx)
    log_f = jnp.maximum(a, bb) + jnp.log1p(jnp.exp(-jnp.abs(a - bb)))
    kk = (1.0 - lb) * _sigmoid(-fx)
    qc = _silu(cq_ref[0])
    iv = ci_ref[0]
    row = lax.broadcasted_iota(I32, (c, 1), 0)
    if valid < c:
        log_f = jnp.where(row < valid, log_f, 0.0)

    rr = lax.broadcasted_iota(I32, (c, c), 0)
    cc = lax.broadcasted_iota(I32, (c, c), 1)
    hs = [slice(h * C_KDIM, (h + 1) * C_KDIM) for h in range(C_HEADS)]
    att = [jnp.zeros((c, c), F32) for _ in range(C_HEADS)]
    cs = log_f
    tot = log_f
    m = 1
    while m < c:
        right = ((row // m) % 2) == 1
        qm = jnp.where(right, qc * jnp.exp(cs), 0.0).astype(BF16)
        km = jnp.where(right, 0.0, kk * jnp.exp(tot - cs)).astype(BF16)
        pair = (rr // (2 * m)) == (cc // (2 * m))
        for h in range(C_HEADS):
            att[h] = att[h] + jnp.where(pair, _dot_nt(qm[:, hs[h]], km[:, hs[h]]), 0.0)
        sib = jnp.where(right, pltpu.roll(tot, m, 0), pltpu.roll(tot, c - m, 0))
        cs = jnp.where(right, cs + sib, cs)
        tot = tot + sib
        m *= 2
    qdec = (qc * jnp.exp(cs)).astype(BF16)
    kdec = (kk * jnp.exp(tot - cs)).astype(BF16)
    ivb = iv.astype(BF16)
    eye_c = rr == cc
    eye_k = (lax.broadcasted_iota(I32, (C_KDIM, C_KDIM), 0)
             == lax.broadcasted_iota(I32, (C_KDIM, C_KDIM), 1))
    ys = []
    for h in range(C_HEADS):
        sl = hs[h]
        diag = jnp.sum(qc[:, sl] * kk[:, sl], axis=1, keepdims=True)
        a_h = att[h] + jnp.where(eye_c, diag, 0.0)
        s_h = s_ref[0, h]
        o = (jnp.dot(a_h.astype(BF16), ivb[:, sl], preferred_element_type=F32)
             + jnp.dot(qdec[:, sl], s_h.astype(BF16), preferred_element_type=F32))
        e_end = jnp.exp(tot[0:1, sl])
        e_col = jnp.sum(jnp.where(eye_k, e_end, 0.0), axis=1, keepdims=True)
        upd = lax.dot_general(kdec[:, sl], ivb[:, sl], (((0,), (0,)), ((), ())),
                              preferred_element_type=F32)
        s_ref[0, h] = e_col * s_h + upd
        ms = jnp.mean(o * o, axis=-1, keepdims=True)
        ys.append(o * lax.rsqrt(ms + EPS) * ng_ref[...])
    yc_ref[0] = jnp.concatenate(ys, axis=1) * _silu(cg_ref[0])


def _hgrn(z3, lb, cng, s0, c, valid):
    b, t, _ = z3.shape
    zblk = lambda name: pl.BlockSpec((1, c, D_C), lambda bi, i, col=_col(name, D_C): (bi, i, col))
    sblk = pl.BlockSpec((1, C_HEADS, C_KDIM, C_VDIM), lambda bi, i: (bi, 0, 0, 0))
    return pl.pallas_call(
        functools.partial(_hgrn_kernel, c=c, valid=valid),
        grid=(b, t // c),
        in_specs=[zblk("cq"), zblk("cf"), zblk("ci"), zblk("c_gate"),
                  pl.BlockSpec((1, C_FDIM), lambda bi, i: (0, 0)),
                  pl.BlockSpec((1, C_VDIM), lambda bi, i: (0, 0)), sblk],
        out_specs=[pl.BlockSpec((1, c, D_C), lambda bi, i: (bi, i, 0)), sblk],
        out_shape=[jax.ShapeDtypeStruct((b, t, D_C), F32),
                   jax.ShapeDtypeStruct((b, C_HEADS, C_KDIM, C_VDIM), F32)],
        compiler_params=_params(("parallel", "arbitrary")),
        name="hgrn",
    )(z3, z3, z3, z3, lb.reshape(1, C_FDIM), cng.reshape(1, C_VDIM), s0)


def _merge_kernel(x_ref, ya_ref, yb_ref, yc_ref, ga_ref, gb_ref, gc_ref, gate_ref,
                  wa_ref, wb_ref, wc_ref, wo_ref, o_ref):
    def proj(y_ref, w_ref):
        return jnp.dot(y_ref[...].astype(BF16), w_ref[...], preferred_element_type=F32)

    m = (_sigmoid(ga_ref[...]) * proj(ya_ref, wa_ref)
         + _sigmoid(gb_ref[...]) * proj(yb_ref, wb_ref)
         + _sigmoid(gc_ref[...]) * proj(yc_ref, wc_ref))
    o_ref[...] = x_ref[...] + gate_ref[0] * jnp.dot(m.astype(BF16), wo_ref[...], preferred_element_type=F32)


def _merge(x2, ya, yb, yc, z, gate3, wa, wb, wc, wo, tm, tiles_per_mod):
    m, d = x2.shape
    r = gate3.shape[1]
    row = lambda w: pl.BlockSpec((tm, w), lambda i: (i, 0))
    zc = lambda name: pl.BlockSpec((tm, d), lambda i, c=_col(name, d): (i, c))
    wspec = lambda k: pl.BlockSpec((k, d), lambda i: (0, 0))
    return pl.pallas_call(
        _merge_kernel,
        grid=(m // tm,),
        in_specs=[row(d), row(D_A), row(D_B), row(D_C), zc("g_a"), zc("g_b"), zc("g_c"),
                  pl.BlockSpec((1, r, d), lambda i: (i // tiles_per_mod, 0, 0)),
                  wspec(D_A), wspec(D_B), wspec(D_C), wspec(d)],
        out_specs=row(d),
        out_shape=jax.ShapeDtypeStruct((m, d), F32),
        compiler_params=_params(("parallel",)),
        name="merge",
    )(x2, ya, yb, yc, z, z, z, gate3, wa, wb, wc, wo)


def _pick_tile(n, pref):
    t = min(pref, n)
    while n % t:
        t //= 2
    return t


def kernel(x_prompt, x_sample, cache_k, cache_v, cache_idx_k, state_conv, state_hgrn, page_table,
           c_prompt, c_sample, w_ada, b_ada, norm_g, w_in, w_dw, b_dw, ln_g, ln_b, q_norm_g, k_norm_g,
           lb_logits, c_norm_g, w_proj_a, w_proj_b, w_proj_c, w_out):
    depth = w_in.shape[0]
    bp, tp, d = x_prompt.shape
    bs, ts, _ = x_sample.shape
    mp, ms = bp * tp, bs * ts
    n_pages = page_table.shape[1]
    past = n_pages * PAGE_SIZE
    assert d == D_MODEL and tp % Q_BLOCK == 0 and tp % CHUNK == 0 and tp >= HALO
    assert ts <= SUBLANES and ms % SUBLANES == 0

    lbp = jax.nn.softmax(lb_logits.astype(F32), axis=0)
    lb_all = jnp.cumsum(lbp, axis=0) - lbp[0:1]
    cos_p, sin_p = _rope_tables(jnp.arange(tp))
    cos_s, sin_s = _rope_tables(past + (jnp.arange(ms) % ts))
    ckt = jnp.transpose(cache_k, (0, 1, 3, 4, 2)).reshape(-1, KV_DIM, PAGE_SIZE)
    cvt = jnp.transpose(cache_v, (0, 1, 3, 4, 2)).reshape(-1, KV_DIM, PAGE_SIZE)
    ckit = jnp.transpose(cache_idx_k, (0, 1, 3, 2)).reshape(-1, IDX_DIM, PAGE_SIZE)
    c_all = jnp.concatenate([c_prompt, c_sample], axis=0)

    tm_p = _pick_tile(mp, 1024)
    tm_p = _pick_tile(tp, tm_p)
    tt = _pick_tile(tp, 256)
    tm_g = _pick_tile(tp, 256)
    zeros_buf = jnp.zeros((bp, HALO, D_A), F32)
    s00 = jnp.zeros((bp, C_HEADS, C_KDIM, C_VDIM), F32)
    pad_rows = lambda a3: jnp.pad(a3, ((0, 0), (0, SUBLANES - ts), (0, 0)))

    xp = x_prompt.reshape(mp, d)
    xs = x_sample.reshape(ms, d)
    outs = [[] for _ in range(10)]
    for l in range(depth):
        mod = _ada(c_all, w_ada[l].astype(BF16), b_ada[l])
        shift, scale, gate = mod[:, :d], mod[:, d:2 * d], mod[:, 2 * d:]
        w_in_t = _pack_w_in_t(w_in[l]).astype(BF16)
        wa, wb, wc, wo = (w_proj_a[l].astype(BF16), w_proj_b[l].astype(BF16),
                          w_proj_c[l].astype(BF16), w_out[l].astype(BF16))
        qg = jnp.tile(q_norm_g[l], LANES // HEAD_DIM).reshape(1, LANES)
        kg = jnp.tile(k_norm_g[l], LANES // HEAD_DIM).reshape(1, LANES)
        lb = lb_all[l]

        per_seq = lambda a: a[:bp].reshape(bp, 1, d)
        h = _prenorm(xp, norm_g[l], per_seq(scale), per_seq(shift), tm_p, tp // tm_p)
        z = _matmul_nt(h, w_in_t, tm_p, NZ // 6)
        z3 = z.reshape(bp, tp, NZ)
        ya, utail = _conv_prompt(z3, zeros_buf, w_dw[l], b_dw[l], ln_g[l], ln_b[l], tt)
        q, k, qi, ki, v, kb, vb, kib = _qkrope(z, cos_p, sin_p, qg, kg, tm_p, tp // tm_p)
        r3 = lambda a: a.reshape(bp, tp, a.shape[-1])
        yb = _dsa_prompt(r3(q), r3(qi), z3, r3(kb), r3(vb), r3(kib))
        yc, s_p = _hgrn(z3, lb, c_norm_g[l], s00, CHUNK, CHUNK)
        xp = _merge(xp, ya.reshape(mp, D_A), yb.reshape(mp, D_B), yc.reshape(mp, D_C), z,
                    per_seq(gate), wa, wb, wc, wo, tm_g, tp // tm_g)
        outs[0].append(k.reshape(bp, tp, N_KV_HEADS, HEAD_DIM))
        outs[1].append(v.reshape(bp, tp, N_KV_HEADS, HEAD_DIM))
        outs[2].append(ki.reshape(bp, tp, IDX_DIM))
        outs[3].append(utail[:, HALO - (CONV_W - 1):, :])
        outs[4].append(s_p)

        per_row = lambda a: jnp.repeat(a[bp:], ts, axis=0).reshape(1, ms, d)
        hs = _prenorm(xs, norm_g[l], per_row(scale), per_row(shift), ms, 1)
        zs = _matmul_nt(hs, w_in_t, ms, NZ // 6)
        zs3 = zs.reshape(bs, ts, NZ)
        seg_t = lambda name: jnp.swapaxes(
            zs3[:, :, _PACK_OFF[name]:_PACK_OFF[name] + D_A], 0, 1)
        ya_t, u_t = _conv_sample(seg_t("a_val"), seg_t("a_glu"), seg_t("a_gate"),
                                 jnp.swapaxes(state_conv[l], 0, 1), w_dw[l], b_dw[l], ln_g[l], ln_b[l])
        ya_s = jnp.swapaxes(ya_t, 0, 1).reshape(ms, D_A)
        q, k, qi, ki, v, _, _, _ = _qkrope(zs, cos_s, sin_s, qg, kg, ms, 1)
        p8 = lambda a: pad_rows(a.reshape(bs, ts, a.shape[-1]))
        zs8 = pad_rows(zs3)
        seg8 = lambda name, w: zs8[:, :, _PACK_OFF[name]:_PACK_OFF[name] + w]
        new_t = lambda a: jnp.pad(jnp.swapaxes(a.reshape(bs, ts, a.shape[-1]), 1, 2),
                                  ((0, 0), (0, 0), (0, LANES - ts)))
        yb8 = _dsa_sample(page_table, p8(q), p8(qi), seg8("zwi", LANES), seg8("b_gate", D_B),
                          new_t(k), new_t(v), new_t(ki), ckt, cvt, ckit, l, depth, ts)
        yc8, s_s = _hgrn(zs8, lb, c_norm_g[l], state_hgrn[l], SUBLANES, ts)
        xs = _merge(xs, ya_s, yb8[:, :ts].reshape(ms, D_B), yc8[:, :ts].reshape(ms, D_C), zs,
                    per_row(gate), wa, wb, wc, wo, ms, 1)
        outs[5].append(k.reshape(bs, ts, N_KV_HEADS, HEAD_DIM))
        outs[6].append(v.reshape(bs, ts, N_KV_HEADS, HEAD_DIM))
        outs[7].append(ki.reshape(bs, ts, IDX_DIM))
        outs[8].append(jnp.concatenate([state_conv[l][:, ts:], jnp.swapaxes(u_t, 0, 1)], axis=1))
        outs[9].append(s_s)

    st = [jnp.stack(o) for o in outs]
    return (xp.reshape(bp, tp, d), xs.reshape(bs, ts, d), st[0], st[1], st[2], st[3], st[4],
            st[5], st[6], st[7], st[8], st[9])
```

```python
import functools
import math

import jax
import jax.numpy as jnp
import numpy as np
from jax import lax
from jax.experimental import pallas as pl
from jax.experimental.pallas import tpu as pltpu

F32 = jnp.float32
BF16 = jnp.bfloat16
I32 = jnp.int32

LANES = 128
SUBLANES = 8
VMEM_LIMIT = 56 * 1024 * 1024

D_MODEL = 1024
EPS = 1e-6
NEG = -1e30
LB_FLOOR = 1e-30
D_A = D_MODEL // 2
CONV_W = 31
HALO = 32
N_HEADS = 8
N_KV_HEADS = 2
HEAD_DIM = 64
D_B = N_HEADS * HEAD_DIM
KV_DIM = N_KV_HEADS * HEAD_DIM
IDX_HEADS = 4
IDX_DIM = 64
TOPK_MAX = 256
ROPE_THETA = 10000.0
Q_BLOCK = 128
C_HEADS = 4
C_KDIM = 128
C_VDIM = D_MODEL // 2 // C_HEADS
D_C = C_HEADS * C_VDIM
C_FDIM = C_HEADS * C_KDIM
CHUNK = 64
PAGE_SIZE = 128

_IN_NAMES = ("a_val", "a_glu", "a_gate", "zq", "zk", "zv", "zqi", "zki", "zwi", "b_gate",
             "cq", "cf", "ci", "c_gate", "g_a", "g_b", "g_c")
_IN_WIDTHS = (D_A, D_A, D_A, D_B, KV_DIM, KV_DIM, IDX_HEADS * IDX_DIM, IDX_DIM, IDX_HEADS, D_B,
              C_FDIM, C_FDIM, D_C, D_C, D_MODEL, D_MODEL, D_MODEL)
_IN_OFFS = dict(zip(_IN_NAMES, np.concatenate([[0], np.cumsum(_IN_WIDTHS)[:-1]]).tolist()))
_IN_W = dict(zip(_IN_NAMES, _IN_WIDTHS))
_PACK_ORDER = ("g_a", "g_b", "g_c", "a_val", "a_glu", "a_gate", "zq", "b_gate", "cq", "cf", "ci",
               "c_gate", "zqi", "zk", "zv", "zki", "zwi")


def _round_up(n, m):
    return (n + m - 1) // m * m


_PACK_W = {n: _round_up(_IN_W[n], LANES) for n in _PACK_ORDER}
_PACK_OFF = {}
_o = 0
for _n in _PACK_ORDER:
    assert _o % _PACK_W[_n] == 0
    _PACK_OFF[_n] = _o
    _o += _PACK_W[_n]
NZ = _o


def _pack_w_in_t(w):
    wt = jnp.swapaxes(w, 0, 1)
    parts = []
    for n in _PACK_ORDER:
        seg = wt[_IN_OFFS[n]:_IN_OFFS[n] + _IN_W[n]]
        pad = _PACK_W[n] - _IN_W[n]
        if pad:
            seg = jnp.pad(seg, ((0, pad), (0, 0)))
        parts.append(seg)
    return jnp.concatenate(parts, axis=0)


def _col(name, width):
    assert _PACK_OFF[name] % width == 0
    return _PACK_OFF[name] // width


def _sigmoid(x):
    return jax.nn.sigmoid(x)


def _silu(x):
    return x * jax.nn.sigmoid(x)


def _params(sem):
    return pltpu.CompilerParams(dimension_semantics=sem, vmem_limit_bytes=VMEM_LIMIT)


def _ada_kernel(c_ref, w_ref, b_ref, o_ref):
    c = c_ref[...]
    o_ref[...] = jnp.dot(_silu(c).astype(BF16), w_ref[...], preferred_element_type=F32) + b_ref[...]


def _ada(c, w_bf, b):
    n, d = c.shape
    nout = w_bf.shape[1]
    tn = D_MODEL
    return pl.pallas_call(
        _ada_kernel,
        grid=(nout // tn,),
        in_specs=[pl.BlockSpec((n, d), lambda j: (0, 0)),
                  pl.BlockSpec((d, tn), lambda j: (0, j)),
                  pl.BlockSpec((1, tn), lambda j: (0, j))],
        out_specs=pl.BlockSpec((n, tn), lambda j: (0, j)),
        out_shape=jax.ShapeDtypeStruct((n, nout), F32),
        compiler_params=_params(("arbitrary",)),
        name="ada",
    )(c, w_bf, b.reshape(1, nout))


def _prenorm_kernel(x_ref, g_ref, sc_ref, sh_ref, o_ref):
    x = x_ref[...]
    ms = jnp.mean(x * x, axis=-1, keepdims=True)
    y = x * lax.rsqrt(ms + EPS) * g_ref[...]
    o_ref[...] = (y * (1.0 + sc_ref[0]) + sh_ref[0]).astype(o_ref.dtype)


def _prenorm(x2, g, scale3, shift3, tm, tiles_per_mod):
    m, d = x2.shape
    r = scale3.shape[1]
    return pl.pallas_call(
        _prenorm_kernel,
        grid=(m // tm,),
        in_specs=[pl.BlockSpec((tm, d), lambda i: (i, 0)),
                  pl.BlockSpec((1, d), lambda i: (0, 0)),
                  pl.BlockSpec((1, r, d), lambda i: (i // tiles_per_mod, 0, 0)),
                  pl.BlockSpec((1, r, d), lambda i: (i // tiles_per_mod, 0, 0))],
        out_specs=pl.BlockSpec((tm, d), lambda i: (i, 0)),
        out_shape=jax.ShapeDtypeStruct((m, d), BF16),
        compiler_params=_params(("parallel",)),
        name="prenorm",
    )(x2, g.reshape(1, d), scale3, shift3)


def _mm_kernel(a_ref, bt_ref, o_ref):
    o_ref[...] = lax.dot_general(a_ref[...], bt_ref[...], (((1,), (1,)), ((), ())),
                                 preferred_element_type=F32)


def _matmul_nt(a_bf, bt_bf, tm, tn):
    m, k = a_bf.shape
    n = bt_bf.shape[0]
    return pl.pallas_call(
        _mm_kernel,
        grid=(n // tn, m // tm),
        in_specs=[pl.BlockSpec((tm, k), lambda j, i: (i, 0)),
                  pl.BlockSpec((tn, k), lambda j, i: (j, 0))],
        out_specs=pl.BlockSpec((tm, tn), lambda j, i: (i, j)),
        out_shape=jax.ShapeDtypeStruct((m, n), F32),
        compiler_params=_params(("parallel", "parallel")),
        name="inproj",
    )(a_bf, bt_bf)


def _ln_swish_gate(y, gate, bdw, lng, lnb):
    y = y + bdw
    mu = jnp.mean(y, axis=-1, keepdims=True)
    yc = y - mu
    var = jnp.mean(yc * yc, axis=-1, keepdims=True)
    yn = yc * lax.rsqrt(var + EPS) * lng + lnb
    return _silu(yn) * _silu(gate)


def _conv_prompt_kernel(val_ref, glu_ref, gate_ref, valh_ref, gluh_ref, buf_ref, w_ref, bdw_ref,
                        lng_ref, lnb_ref, ya_ref, utail_ref, f_ref, g_ref, *, tt):
    ti = pl.program_id(1)
    u = val_ref[0] * _sigmoid(glu_ref[0])
    uh = valh_ref[0] * _sigmoid(gluh_ref[0])
    f_ref[0:HALO, :] = jnp.where(ti == 0, buf_ref[0], uh)
    f_ref[HALO:HALO + tt, :] = u
    nrows = g_ref.shape[1]
    for p in range(1, SUBLANES):
        g_ref[p - 1] = f_ref[pl.ds(p, nrows), :]
    off = HALO - (CONV_W - 1)
    acc = jnp.zeros((tt, D_A), F32)
    for j in range(CONV_W):
        p = (off + j) % SUBLANES
        a = off + j - p
        win = f_ref[pl.ds(a, tt), :] if p == 0 else g_ref[p - 1, pl.ds(a, tt), :]
        acc = acc + w_ref[j:j + 1, :] * win
    ya_ref[0] = _ln_swish_gate(acc, gate_ref[0], bdw_ref[...], lng_ref[...], lnb_ref[...])

    @pl.when(ti == pl.num_programs(1) - 1)
    def _():
        utail_ref[0] = f_ref[tt:tt + HALO, :]


def _conv_prompt(z3, buf32, w_dw, b_dw, ln_g, ln_b, tt):
    b, t, _ = z3.shape
    hb = tt // HALO
    cur = lambda name: pl.BlockSpec((1, tt, D_A), lambda bi, ti, c=_col(name, D_A): (bi, ti, c))
    halo = lambda name: pl.BlockSpec(
        (1, HALO, D_A), lambda bi, ti, c=_col(name, D_A): (bi, jnp.maximum(ti * hb - 1, 0), c))
    vec = pl.BlockSpec((1, D_A), lambda bi, ti: (0, 0))
    wpad = jnp.pad(w_dw, ((0, HALO - CONV_W), (0, 0)))
    return pl.pallas_call(
        functools.partial(_conv_prompt_kernel, tt=tt),
        grid=(b, t // tt),
        in_specs=[cur("a_val"), cur("a_glu"), cur("a_gate"), halo("a_val"), halo("a_glu"),
                  pl.BlockSpec((1, HALO, D_A), lambda bi, ti: (bi, 0, 0)),
                  pl.BlockSpec((HALO, D_A), lambda bi, ti: (0, 0)), vec, vec, vec],
        out_specs=[pl.BlockSpec((1, tt, D_A), lambda bi, ti: (bi, ti, 0)),
                   pl.BlockSpec((1, HALO, D_A), lambda bi, ti: (bi, 0, 0))],
        out_shape=[jax.ShapeDtypeStruct((b, t, D_A), F32),
                   jax.ShapeDtypeStruct((b, HALO, D_A), F32)],
        scratch_shapes=[pltpu.VMEM((HALO + tt, D_A), F32),
                        pltpu.VMEM((SUBLANES - 1, HALO + tt - SUBLANES, D_A), F32)],
        compiler_params=_params(("parallel", "arbitrary")),
        name="conv_prompt",
    )(z3, z3, z3, z3, z3, buf32, wpad, b_dw.reshape(1, D_A), ln_g.reshape(1, D_A), ln_b.reshape(1, D_A))


def _conv_sample_kernel(val_ref, glu_ref, gate_ref, buf_ref, w_ref, bdw_ref, lng_ref, lnb_ref,
                        ya_ref, u_ref, *, ts):
    nb = CONV_W - 1
    for t in range(ts):
        u_ref[t] = val_ref[t] * _sigmoid(glu_ref[t])
    for t in range(ts):
        acc = jnp.zeros(u_ref.shape[1:], F32)
        for j in range(CONV_W):
            r = t + j
            src = buf_ref[r] if r < nb else u_ref[r - nb]
            acc = acc + w_ref[j:j + 1, :] * src
        ya_ref[t] = _ln_swish_gate(acc, gate_ref[t], bdw_ref[...], lng_ref[...], lnb_ref[...])


def _conv_sample(val_t, glu_t, gate_t, buf_t, w_dw, b_dw, ln_g, ln_b):
    ts, b, _ = val_t.shape
    assert ts <= CONV_W - 1
    return pl.pallas_call(
        functools.partial(_conv_sample_kernel, ts=ts),
        out_shape=[jax.ShapeDtypeStruct((ts, b, D_A), F32), jax.ShapeDtypeStruct((ts, b, D_A), F32)],
        compiler_params=pltpu.CompilerParams(vmem_limit_bytes=VMEM_LIMIT),
        name="conv_sample",
    )(val_t, glu_t, gate_t, buf_t, w_dw, b_dw.reshape(1, D_A), ln_g.reshape(1, D_A), ln_b.reshape(1, D_A))


def _lane_iota(shape):
    return lax.broadcasted_iota(I32, shape, len(shape) - 1)


def _swap_half(x):
    w = x.shape[-1]
    half = HEAD_DIM // 2
    first = (_lane_iota(x.shape) % HEAD_DIM) < half
    return jnp.where(first, pltpu.roll(x, w - half, 1), pltpu.roll(x, half, 1))


def _group_sum(s):
    same_group = (lax.broadcasted_iota(I32, (LANES, LANES), 0) // HEAD_DIM
                  == lax.broadcasted_iota(I32, (LANES, LANES), 1) // HEAD_DIM).astype(BF16)
    hi = s.astype(BF16)
    r1 = s - hi.astype(F32)
    mid = r1.astype(BF16)
    lo = (r1 - mid.astype(F32)).astype(BF16)
    cols = []
    for c in range(s.shape[-1] // LANES):
        sl = slice(c * LANES, (c + 1) * LANES)
        cols.append(jnp.dot(hi[:, sl], same_group, preferred_element_type=F32)
                    + jnp.dot(mid[:, sl], same_group, preferred_element_type=F32)
                    + jnp.dot(lo[:, sl], same_group, preferred_element_type=F32))
    return cols[0] if len(cols) == 1 else jnp.concatenate(cols, axis=-1)


def _tile_lanes(t, w):
    reps = w // t.shape[-1]
    return t if reps == 1 else jnp.concatenate([t] * reps, axis=-1)


def _rope(x, cos, sin):
    w = x.shape[-1]
    return x * _tile_lanes(cos, w) + _swap_half(x) * _tile_lanes(sin, w)


def _head_rms(x, g):
    ms = _group_sum(x * x) * (1.0 / HEAD_DIM)
    y = x * lax.rsqrt(ms + EPS)
    return y if g is None else y * _tile_lanes(g, x.shape[-1])


def _qkrope_kernel(zq_ref, zk_ref, zqi_ref, zki_ref, zv_ref, cos_ref, sin_ref, qg_ref, kg_ref,
                   q_ref, k_ref, qi_ref, ki_ref, v_ref, kb_ref, vb_ref, kib_ref):
    cos = cos_ref[...]
    sin = sin_ref[...]
    q_ref[...] = _rope(_head_rms(zq_ref[...], qg_ref[...]), cos, sin)
    k = _rope(_head_rms(zk_ref[...], kg_ref[...]), cos, sin)
    k_ref[...] = k
    kb_ref[...] = k.astype(BF16)
    qi_ref[...] = _rope(zqi_ref[...], cos, sin)
    ki = _rope(_head_rms(zki_ref[...], None), cos, sin)[:, :IDX_DIM]
    ki_ref[...] = ki
    kib_ref[...] = ki.astype(BF16)
    v = zv_ref[...]
    v_ref[...] = v
    vb_ref[...] = v.astype(BF16)


def _qkrope(z, cos_t, sin_t, qg, kg, tm, table_tiles):
    m = z.shape[0]
    zc = lambda name, w: pl.BlockSpec((tm, w), lambda i, c=_col(name, w): (i, c))
    tab = pl.BlockSpec((tm, LANES), lambda i: (i % table_tiles, 0))
    vec = pl.BlockSpec((1, LANES), lambda i: (0, 0))
    row = lambda w: pl.BlockSpec((tm, w), lambda i: (i, 0))
    qi_w = IDX_HEADS * IDX_DIM
    return pl.pallas_call(
        _qkrope_kernel,
        grid=(m // tm,),
        in_specs=[zc("zq", D_B), zc("zk", KV_DIM), zc("zqi", qi_w), zc("zki", LANES), zc("zv", KV_DIM),
                  tab, tab, vec, vec],
        out_specs=[row(D_B), row(KV_DIM), row(qi_w), row(IDX_DIM), row(KV_DIM),
                   row(KV_DIM), row(KV_DIM), row(IDX_DIM)],
        out_shape=[jax.ShapeDtypeStruct((m, D_B), F32), jax.ShapeDtypeStruct((m, KV_DIM), F32),
                   jax.ShapeDtypeStruct((m, qi_w), F32), jax.ShapeDtypeStruct((m, IDX_DIM), F32),
                   jax.ShapeDtypeStruct((m, KV_DIM), F32),
                   jax.ShapeDtypeStruct((m, KV_DIM), BF16), jax.ShapeDtypeStruct((m, KV_DIM), BF16),
                   jax.ShapeDtypeStruct((m, IDX_DIM), BF16)],
        compiler_params=_params(("parallel",)),
        name="qkrope",
    )(z, z, z, z, z, cos_t, sin_t, qg, kg)


def _rope_tables(pos):
    half = HEAD_DIM // 2
    inv = ROPE_THETA ** (-jnp.arange(half, dtype=F32) / half)
    ang = pos.astype(F32)[:, None] * inv[None, :]
    cos = jnp.cos(ang)
    sin = jnp.sin(ang)
    cos64 = jnp.concatenate([cos, cos], axis=1)
    sin64 = jnp.concatenate([-sin, sin], axis=1)
    reps = LANES // HEAD_DIM
    return jnp.tile(cos64, (1, reps)), jnp.tile(sin64, (1, reps))


_SIGN = np.int32(-2 ** 31)
_MAG = np.int32(0x7FFFFFFF)
QK_SCALE = HEAD_DIM ** -0.5 * math.log2(math.e)
SEARCH_UNROLL = 4
SAMPLE_DIGIT_BITS = 3


def _dot_nt(a, b):
    return lax.dot_general(a, b, (((1,), (1,)), ((), ())), preferred_element_type=F32)


def _row_count(mask):
    return jnp.sum(mask.astype(F32), axis=1, keepdims=True)


def _ordered_to_float(t):
    key = t ^ _SIGN
    return lax.bitcast_convert_type(jnp.where(key < 0, key ^ _MAG, key), F32)


def _threshold_bits(sc_ref, kf):
    nq, nl = sc_ref.shape

    def cond(c):
        i, _, cnt_t = c
        return (i < 32) & (jnp.max(jnp.abs(cnt_t - kf)) > 0.0)

    def body(c):
        i, t, cnt_t = c
        for j in range(SEARCH_UNROLL):
            cand = t | jnp.left_shift(jnp.int32(1), 31 - (i + j))
            cnt = _row_count(sc_ref[...] >= _ordered_to_float(cand))
            ok = cnt >= kf
            t = jnp.where(ok, cand, t)
            cnt_t = jnp.where(ok, cnt, cnt_t)
        return i + SEARCH_UNROLL, t, cnt_t

    init = (jnp.int32(0), jnp.zeros((nq, 1), I32), jnp.full((nq, 1), float(nl), F32))
    return lax.while_loop(cond, body, init)[1]


def _threshold_digits(sc_ref, kf, digit_bits):
    nq, _ = sc_ref.shape
    t = jnp.zeros((nq, 1), I32)
    pos = 32
    while pos > 0:
        nb = (pos % digit_bits) or digit_bits
        pos -= nb
        digit = jnp.zeros((nq, 1), I32)
        for j in range(1, 2 ** nb):
            cand = t | np.uint32(j << pos).astype(np.int32)
            ok = _row_count(sc_ref[...] >= _ordered_to_float(cand)) >= kf
            digit = digit + ok.astype(I32)
        t = t | jnp.left_shift(digit, pos)
    return t


def _select_topk(sc_ref, sel_ref, topk, digit_bits=1):
    nq, nl = sc_ref.shape
    if nl == topk:
        sel_ref[...] = jnp.ones((nq, nl), F32)
        return
    kf = float(topk)
    t = _threshold_bits(sc_ref, kf) if digit_bits == 1 else _threshold_digits(sc_ref, kf, digit_bits)
    tau = jnp.where(t == 0, -jnp.inf, _ordered_to_float(t))
    sc = sc_ref[...]
    gt = sc > tau
    eq = sc == tau
    need = kf - _row_count(gt)
    sel_ref[...] = (gt | eq).astype(F32)
    tie_rows = (_row_count(eq) > need) & (tau > NEG)

    @pl.when(jnp.max(tie_rows.astype(F32)) > 0.0)
    def _():
        tri = (lax.broadcasted_iota(I32, (LANES, LANES), 0)
               < lax.broadcasted_iota(I32, (LANES, LANES), 1)).astype(BF16)
        run = jnp.zeros((nq, 1), F32)
        for j in range(nl // LANES):
            sl = slice(j * LANES, (j + 1) * LANES)
            sj = sc_ref[:, sl]
            eqj = sj == tau
            rank = jnp.dot(eqj.astype(BF16), tri, preferred_element_type=F32) + run
            sel_ref[:, sl] = ((sj > tau) | (eqj & (rank < need))).astype(F32)
            run = run + _row_count(eqj)


def _dsa_core(q, qi, wi, bgate, qpos, kb_ref, vb_ref, kib_ref, sc_ref, sel_ref, topk):
    nq = q.shape[0]
    nl = kb_ref.shape[0]
    kib = kib_ref[...]
    qis = (qi * (IDX_DIM ** -0.5)).astype(BF16)
    wis = wi * (IDX_HEADS ** -0.5)
    scores = jnp.zeros((nq, nl), F32)
    for h in range(IDX_HEADS):
        s = _dot_nt(qis[:, h * IDX_DIM:(h + 1) * IDX_DIM], kib)
        scores = scores + jnp.maximum(s, 0.0) * wis[:, h:h + 1]
    causal = _lane_iota((nq, nl)) <= qpos
    sc_ref[...] = jnp.where(causal, scores, NEG)
    _select_topk(sc_ref, sel_ref, topk)

    sel = (sel_ref[...] > 0.0) & causal
    lane = _lane_iota((nq, LANES))
    group_w = N_HEADS // N_KV_HEADS
    kb = kb_ref[...]
    vb = vb_ref[...]
    qs = q * QK_SCALE
    outs = []
    for h in range(N_HEADS):
        g = h // group_w
        c = (h * HEAD_DIM) // LANES
        x = qs[:, c * LANES:(c + 1) * LANES]
        if (h % 2) != g:
            x = pltpu.roll(x, HEAD_DIM, 1)
        in_g = (lane // HEAD_DIM) == g
        xq = jnp.where(in_g, x, 0.0).astype(BF16)
        s = jnp.where(sel, _dot_nt(xq, kb), NEG)
        mx = jnp.max(s, axis=1, keepdims=True)
        p = jnp.exp2(s - mx)
        den = jnp.sum(p, axis=1, keepdims=True)
        o = jnp.dot(p.astype(BF16), vb, preferred_element_type=F32) / den
        o = jnp.where(in_g, o, 0.0)
        if (h % 2) != g:
            o = pltpu.roll(o, HEAD_DIM, 1)
        outs.append(o)
    cols = [outs[2 * c] + outs[2 * c + 1] for c in range(N_HEADS // 2)]
    return jnp.concatenate(cols, axis=1) * _silu(bgate)


def _dsa_prompt_kernel(q_ref, qi_ref, wi_ref, bg_ref, kb_ref, vb_ref, kib_ref, o_ref, sc_ref, sel_ref,
                       *, topk, q_lo):
    qb = pl.program_id(1) + q_lo
    nq = q_ref.shape[1]
    qpos = qb * nq + lax.broadcasted_iota(I32, (nq, 1), 0)
    o_ref[0] = _dsa_core(q_ref[0], qi_ref[0], wi_ref[0], bg_ref[0], qpos,
                         kb_ref.at[0], vb_ref.at[0], kib_ref.at[0], sc_ref, sel_ref, topk)


def _dsa_prompt_bucket(q3, qi3, z3, kb3, vb3, kib3, nq, q_lo, q_hi):
    b, t, _ = q3.shape
    topk = min(TOPK_MAX, t // 4)
    nl = q_hi * nq
    qi_w = IDX_HEADS * IDX_DIM
    qblk = lambda w: pl.BlockSpec((1, nq, w), lambda bi, i: (bi, i + q_lo, 0))
    zblk = lambda name, w: pl.BlockSpec((1, nq, w), lambda bi, i, c=_col(name, w): (bi, i + q_lo, c))
    keys = lambda w: pl.BlockSpec((1, nl, w), lambda bi, i: (bi, 0, 0))
    return pl.pallas_call(
        functools.partial(_dsa_prompt_kernel, topk=topk, q_lo=q_lo),
        grid=(b, q_hi - q_lo),
        in_specs=[qblk(D_B), qblk(qi_w), zblk("zwi", LANES), zblk("b_gate", D_B),
                  keys(KV_DIM), keys(KV_DIM), keys(IDX_DIM)],
        out_specs=pl.BlockSpec((1, nq, D_B), lambda bi, i: (bi, i, 0)),
        out_shape=jax.ShapeDtypeStruct((b, (q_hi - q_lo) * nq, D_B), F32),
        scratch_shapes=[pltpu.VMEM((nq, nl), F32), pltpu.VMEM((nq, nl), F32)],
        compiler_params=_params(("parallel", "parallel")),
        name="dsa_prompt",
    )(q3, qi3, z3, z3, kb3, vb3, kib3)


DSA_BUCKETS = 8
DSA_ROWS = 256


def _dsa_prompt(q3, qi3, z3, kb3, vb3, kib3):
    t = q3.shape[1]
    nq = DSA_ROWS if t % DSA_ROWS == 0 else Q_BLOCK
    nblk = t // nq
    step = max(1, nblk // DSA_BUCKETS)
    parts = [_dsa_prompt_bucket(q3, qi3, z3, kb3, vb3, kib3, nq, lo, min(lo + step, nblk))
             for lo in range(0, nblk, step)]
    return parts[0] if len(parts) == 1 else jnp.concatenate(parts, axis=1)


def _dsa_sample_kernel(pt_ref, q_ref, qi_ref, wi_ref, bg_ref, knt_ref, vnt_ref, kint_ref,
                       ck_hbm, cv_hbm, cki_hbm, o_ref, kt_ref, vt_ref, kit_ref, sc_ref, sel_ref, sems,
                       *, topk, n_pages, page_base, ts):
    bi = pl.program_id(0)
    nb = pl.num_programs(0)
    nq = q_ref.shape[1]
    past = n_pages * PAGE_SIZE
    nl = kt_ref.shape[2]
    group_w = N_HEADS // N_KV_HEADS
    slot = bi % 2

    def copies(seq, s, p):
        page = pt_ref[seq, p] + page_base
        dst = pl.ds(pl.multiple_of(p * PAGE_SIZE, PAGE_SIZE), PAGE_SIZE)
        return (pltpu.make_async_copy(ck_hbm.at[page], kt_ref.at[s, :, dst], sems.at[s, 0]),
                pltpu.make_async_copy(cv_hbm.at[page], vt_ref.at[s, :, dst], sems.at[s, 1]),
                pltpu.make_async_copy(cki_hbm.at[page], kit_ref.at[s, :, dst], sems.at[s, 2]))

    def start_gather(seq, s):
        def start(p, carry):
            for cp in copies(seq, s, p):
                cp.start()
            return carry
        lax.fori_loop(0, n_pages, start, 0)

    @pl.when(bi == 0)
    def _():
        start_gather(0, 0)

    @pl.when(bi + 1 < nb)
    def _():
        start_gather(bi + 1, 1 - slot)

    kt_ref[slot, :, past:nl] = knt_ref[0]
    vt_ref[slot, :, past:nl] = vnt_ref[0]
    kit_ref[slot, :, past:nl] = kint_ref[0]

    def wait(p, carry):
        for cp in copies(bi, slot, p):
            cp.wait()
        return carry

    lax.fori_loop(0, n_pages, wait, 0)
    kt_ref, vt_ref, kit_ref = kt_ref.at[slot], vt_ref.at[slot], kit_ref.at[slot]

    row = lax.broadcasted_iota(I32, (nq, 1), 0)
    qpos = past + jnp.minimum(row, ts - 1)
    stack = lambda x, w, heads: jnp.concatenate([x[:, h * w:(h + 1) * w] for h in heads], axis=0)

    qis = stack(qi_ref[0] * (IDX_DIM ** -0.5), IDX_DIM, range(IDX_HEADS)).astype(BF16)
    s = jnp.dot(qis, kit_ref[...].astype(BF16), preferred_element_type=F32)
    wis = wi_ref[0] * (IDX_HEADS ** -0.5)
    scores = jnp.zeros((nq, nl), F32)
    for h in range(IDX_HEADS):
        scores = scores + jnp.maximum(s[h * nq:(h + 1) * nq], 0.0) * wis[:, h:h + 1]
    causal = _lane_iota((nq, nl)) <= qpos
    sc_ref[...] = jnp.where(causal, scores, NEG)
    _select_topk(sc_ref, sel_ref, topk, digit_bits=SAMPLE_DIGIT_BITS)

    sel = (sel_ref[...] > 0.0) & causal
    sel_g = jnp.concatenate([sel] * group_w, axis=0)
    qs = q_ref[0] * QK_SCALE
    outs = []
    for g in range(N_KV_HEADS):
        rows = slice(g * HEAD_DIM, (g + 1) * HEAD_DIM)
        qg = stack(qs, HEAD_DIM, range(g * group_w, (g + 1) * group_w)).astype(BF16)
        sg = jnp.dot(qg, kt_ref[rows, :].astype(BF16), preferred_element_type=F32)
        sg = jnp.where(sel_g, sg, NEG)
        mx = jnp.max(sg, axis=1, keepdims=True)
        p = jnp.exp2(sg - mx)
        den = jnp.sum(p, axis=1, keepdims=True)
        og = _dot_nt(p.astype(BF16), vt_ref[rows, :].astype(BF16)) / den
        outs += [og[j * nq:(j + 1) * nq] for j in range(group_w)]
    o_ref[0] = jnp.concatenate(outs, axis=1) * _silu(bg_ref[0])


def _dsa_sample(page_table, q8, qi8, wi8, bg8, knt, vnt, kint, ckt, cvt, ckit, layer, depth, ts):
    b, nq, _ = q8.shape
    n_pages = page_table.shape[1]
    past = n_pages * PAGE_SIZE
    nl = past + LANES
    topk = min(TOPK_MAX, (past + ts) // 4)
    n_pool = ckt.shape[0] // depth
    qi_w = IDX_HEADS * IDX_DIM
    blk = lambda r, w: pl.BlockSpec((1, r, w), lambda bi, pt: (bi, 0, 0))
    anyspec = pl.BlockSpec(memory_space=pl.ANY)
    grid_spec = pltpu.PrefetchScalarGridSpec(
        num_scalar_prefetch=1,
        grid=(b,),
        in_specs=[blk(nq, D_B), blk(nq, qi_w), blk(nq, LANES), blk(nq, D_B),
                  blk(KV_DIM, LANES), blk(KV_DIM, LANES), blk(IDX_DIM, LANES),
                  anyspec, anyspec, anyspec],
        out_specs=pl.BlockSpec((1, nq, D_B), lambda bi, pt: (bi, 0, 0)),
        scratch_shapes=[pltpu.VMEM((2, KV_DIM, nl), F32), pltpu.VMEM((2, KV_DIM, nl), F32),
                        pltpu.VMEM((2, IDX_DIM, nl), F32),
                        pltpu.VMEM((nq, nl), F32), pltpu.VMEM((nq, nl), F32),
                        pltpu.SemaphoreType.DMA((2, 3))])
    return pl.pallas_call(
        functools.partial(_dsa_sample_kernel, topk=topk, n_pages=n_pages, page_base=layer * n_pool, ts=ts),
        grid_spec=grid_spec,
        out_shape=jax.ShapeDtypeStruct((b, nq, D_B), F32),
        compiler_params=_params(("arbitrary",)),
        name="dsa_sample",
    )(page_table, q8, qi8, wi8, bg8, knt, vnt, kint, ckt, cvt, ckit)


def _hgrn_kernel(cq_ref, cf_ref, ci_ref, cg_ref, lb_ref, ng_ref, s0_ref, yc_ref, s_ref, *, c, valid):
    ci_idx = pl.program_id(1)

    @pl.when(ci_idx == 0)
    def _():
        s_ref[...] = s0_ref[...]

    fx = cf_ref[0]
    lb = lb_ref[...]
    log_f = jnp.log(jnp.maximum(lb, LB_FLOOR) + (1.0 - lb) * _sigmoid(fx))
    kk = (1.0 - lb) * _sigmoid(-fx)
    qc = _silu(cq_ref[0])
    iv = ci_ref[0]
    row = lax.broadcasted_iota(I32, (c, 1), 0)
    if valid < c:
        log_f = jnp.where(row < valid, log_f, 0.0)

    rr = lax.broadcasted_iota(I32, (c, c), 0)
    cc = lax.broadcasted_iota(I32, (c, c), 1)
    hs = [slice(h * C_KDIM, (h + 1) * C_KDIM) for h in range(C_HEADS)]
    att = [jnp.zeros((c, c), F32) for _ in range(C_HEADS)]
    cs = log_f
    tot = log_f
    m = 1
    while m < c:
        right = ((row // m) % 2) == 1
        qm = jnp.where(right, qc * jnp.exp(cs), 0.0).astype(BF16)
        km = jnp.where(right, 0.0, kk * jnp.exp(tot - cs)).astype(BF16)
        pair = (rr // (2 * m)) == (cc // (2 * m))
        for h in range(C_HEADS):
            att[h] = att[h] + jnp.where(pair, _dot_nt(qm[:, hs[h]], km[:, hs[h]]), 0.0)
        sib = jnp.where(right, pltpu.roll(tot, m, 0), pltpu.roll(tot, c - m, 0))
        cs = jnp.where(right, cs + sib, cs)
        tot = tot + sib
        m *= 2
    qdec = (qc * jnp.exp(cs)).astype(BF16)
    kdec = (kk * jnp.exp(tot - cs)).astype(BF16)
    ivb = iv.astype(BF16)
    eye_c = rr == cc
    eye_k = (lax.broadcasted_iota(I32, (C_KDIM, C_KDIM), 0)
             == lax.broadcasted_iota(I32, (C_KDIM, C_KDIM), 1))
    ys = []
    for h in range(C_HEADS):
        sl = hs[h]
        diag = jnp.sum(qc[:, sl] * kk[:, sl], axis=1, keepdims=True)
        a_h = att[h] + jnp.where(eye_c, diag, 0.0)
        s_h = s_ref[0, h]
        o = (jnp.dot(a_h.astype(BF16), ivb[:, sl], preferred_element_type=F32)
             + jnp.dot(qdec[:, sl], s_h.astype(BF16), preferred_element_type=F32))
        e_end = jnp.exp(tot[0:1, sl])
        e_col = jnp.sum(jnp.where(eye_k, e_end, 0.0), axis=1, keepdims=True)
        upd = lax.dot_general(kdec[:, sl], ivb[:, sl], (((0,), (0,)), ((), ())),
                              preferred_element_type=F32)
        s_ref[0, h] = e_col * s_h + upd
        ms = jnp.mean(o * o, axis=-1, keepdims=True)
        ys.append(o * lax.rsqrt(ms + EPS) * ng_ref[...])
    yc_ref[0] = jnp.concatenate(ys, axis=1) * _silu(cg_ref[0])


def _hgrn(z3, lb, cng, s0, c, valid):
    b, t, _ = z3.shape
    zblk = lambda name: pl.BlockSpec((1, c, D_C), lambda bi, i, col=_col(name, D_C): (bi, i, col))
    sblk = pl.BlockSpec((1, C_HEADS, C_KDIM, C_VDIM), lambda bi, i: (bi, 0, 0, 0))
    return pl.pallas_call(
        functools.partial(_hgrn_kernel, c=c, valid=valid),
        grid=(b, t // c),
        in_specs=[zblk("cq"), zblk("cf"), zblk("ci"), zblk("c_gate"),
                  pl.BlockSpec((1, C_FDIM), lambda bi, i: (0, 0)),
                  pl.BlockSpec((1, C_VDIM), lambda bi, i: (0, 0)), sblk],
        out_specs=[pl.BlockSpec((1, c, D_C), lambda bi, i: (bi, i, 0)), sblk],
        out_shape=[jax.ShapeDtypeStruct((b, t, D_C), F32),
                   jax.ShapeDtypeStruct((b, C_HEADS, C_KDIM, C_VDIM), F32)],
        compiler_params=_params(("parallel", "arbitrary")),
        name="hgrn",
    )(z3, z3, z3, z3, lb.reshape(1, C_FDIM), cng.reshape(1, C_VDIM), s0)


def _merge_kernel(x_ref, ya_ref, yb_ref, yc_ref, ga_ref, gb_ref, gc_ref, gate_ref,
                  wa_ref, wb_ref, wc_ref, wo_ref, o_ref):
    def proj(y_ref, w_ref):
        return jnp.dot(y_ref[...].astype(BF16), w_ref[...], preferred_element_type=F32)

    m = (_sigmoid(ga_ref[...]) * proj(ya_ref, wa_ref)
         + _sigmoid(gb_ref[...]) * proj(yb_ref, wb_ref)
         + _sigmoid(gc_ref[...]) * proj(yc_ref, wc_ref))
    o_ref[...] = x_ref[...] + gate_ref[0] * jnp.dot(m.astype(BF16), wo_ref[...], preferred_element_type=F32)


def _merge(x2, ya, yb, yc, z, gate3, wa, wb, wc, wo, tm, tiles_per_mod):
    m, d = x2.shape
    r = gate3.shape[1]
    row = lambda w: pl.BlockSpec((tm, w), lambda i: (i, 0))
    zc = lambda name: pl.BlockSpec((tm, d), lambda i, c=_col(name, d): (i, c))
    wspec = lambda k: pl.BlockSpec((k, d), lambda i: (0, 0))
    return pl.pallas_call(
        _merge_kernel,
        grid=(m // tm,),
        in_specs=[row(d), row(D_A), row(D_B), row(D_C), zc("g_a"), zc("g_b"), zc("g_c"),
                  pl.BlockSpec((1, r, d), lambda i: (i // tiles_per_mod, 0, 0)),
                  wspec(D_A), wspec(D_B), wspec(D_C), wspec(d)],
        out_specs=row(d),
        out_shape=jax.ShapeDtypeStruct((m, d), F32),
        compiler_params=_params(("parallel",)),
        name="merge",
    )(x2, ya, yb, yc, z, z, z, gate3, wa, wb, wc, wo)


def _pick_tile(n, pref):
    t = min(pref, n)
    while n % t:
        t //= 2
    return t


def kernel(x_prompt, x_sample, cache_k, cache_v, cache_idx_k, state_conv, state_hgrn, page_table,
           c_prompt, c_sample, w_ada, b_ada, norm_g, w_in, w_dw, b_dw, ln_g, ln_b, q_norm_g, k_norm_g,
           lb_logits, c_norm_g, w_proj_a, w_proj_b, w_proj_c, w_out):
    depth = w_in.shape[0]
    bp, tp, d = x_prompt.shape
    bs, ts, _ = x_sample.shape
    mp, ms = bp * tp, bs * ts
    n_pages = page_table.shape[1]
    past = n_pages * PAGE_SIZE
    assert d == D_MODEL and tp % Q_BLOCK == 0 and tp % CHUNK == 0 and tp >= HALO
    assert ts <= SUBLANES and ms % SUBLANES == 0

    lbp = jax.nn.softmax(lb_logits.astype(F32), axis=0)
    lb_all = jnp.cumsum(lbp, axis=0) - lbp[0:1]
    cos_p, sin_p = _rope_tables(jnp.arange(tp))
    cos_s, sin_s = _rope_tables(past + (jnp.arange(ms) % ts))
    ckt = jnp.transpose(cache_k, (0, 1, 3, 4, 2)).reshape(-1, KV_DIM, PAGE_SIZE)
    cvt = jnp.transpose(cache_v, (0, 1, 3, 4, 2)).reshape(-1, KV_DIM, PAGE_SIZE)
    ckit = jnp.transpose(cache_idx_k, (0, 1, 3, 2)).reshape(-1, IDX_DIM, PAGE_SIZE)
    c_all = jnp.concatenate([c_prompt, c_sample], axis=0)

    tm_p = _pick_tile(mp, 1024)
    tm_p = _pick_tile(tp, tm_p)
    tt = _pick_tile(tp, 256)
    tm_g = _pick_tile(tp, 256)
    zeros_buf = jnp.zeros((bp, HALO, D_A), F32)
    s00 = jnp.zeros((bp, C_HEADS, C_KDIM, C_VDIM), F32)
    pad_rows = lambda a3: jnp.pad(a3, ((0, 0), (0, SUBLANES - ts), (0, 0)))

    xp = x_prompt.reshape(mp, d)
    xs = x_sample.reshape(ms, d)
    outs = [[] for _ in range(10)]
    for l in range(depth):
        mod = _ada(c_all, w_ada[l].astype(BF16), b_ada[l])
        shift, scale, gate = mod[:, :d], mod[:, d:2 * d], mod[:, 2 * d:]
        w_in_t = _pack_w_in_t(w_in[l]).astype(BF16)
        wa, wb, wc, wo = (w_proj_a[l].astype(BF16), w_proj_b[l].astype(BF16),
                          w_proj_c[l].astype(BF16), w_out[l].astype(BF16))
        qg = jnp.tile(q_norm_g[l], LANES // HEAD_DIM).reshape(1, LANES)
        kg = jnp.tile(k_norm_g[l], LANES // HEAD_DIM).reshape(1, LANES)
        lb = lb_all[l]

        per_seq = lambda a: a[:bp].reshape(bp, 1, d)
        h = _prenorm(xp, norm_g[l], per_seq(scale), per_seq(shift), tm_p, tp // tm_p)
        z = _matmul_nt(h, w_in_t, tm_p, NZ // 6)
        z3 = z.reshape(bp, tp, NZ)
        ya, utail = _conv_prompt(z3, zeros_buf, w_dw[l], b_dw[l], ln_g[l], ln_b[l], tt)
        q, k, qi, ki, v, kb, vb, kib = _qkrope(z, cos_p, sin_p, qg, kg, tm_p, tp // tm_p)
        r3 = lambda a: a.reshape(bp, tp, a.shape[-1])
        yb = _dsa_prompt(r3(q), r3(qi), z3, r3(kb), r3(vb), r3(kib))
        yc, s_p = _hgrn(z3, lb, c_norm_g[l], s00, CHUNK, CHUNK)
        xp = _merge(xp, ya.reshape(mp, D_A), yb.reshape(mp, D_B), yc.reshape(mp, D_C), z,
                    per_seq(gate), wa, wb, wc, wo, tm_g, tp // tm_g)
        outs[0].append(k.reshape(bp, tp, N_KV_HEADS, HEAD_DIM))
        outs[1].append(v.reshape(bp, tp, N_KV_HEADS, HEAD_DIM))
        outs[2].append(ki.reshape(bp, tp, IDX_DIM))
        outs[3].append(utail[:, HALO - (CONV_W - 1):, :])
        outs[4].append(s_p)

        per_row = lambda a: jnp.repeat(a[bp:], ts, axis=0).reshape(1, ms, d)
        hs = _prenorm(xs, norm_g[l], per_row(scale), per_row(shift), ms, 1)
        zs = _matmul_nt(hs, w_in_t, ms, NZ // 6)
        zs3 = zs.reshape(bs, ts, NZ)
        seg_t = lambda name: jnp.swapaxes(
            zs3[:, :, _PACK_OFF[name]:_PACK_OFF[name] + D_A], 0, 1)
        ya_t, u_t = _conv_sample(seg_t("a_val"), seg_t("a_glu"), seg_t("a_gate"),
                                 jnp.swapaxes(state_conv[l], 0, 1), w_dw[l], b_dw[l], ln_g[l], ln_b[l])
        ya_s = jnp.swapaxes(ya_t, 0, 1).reshape(ms, D_A)
        q, k, qi, ki, v, _, _, _ = _qkrope(zs, cos_s, sin_s, qg, kg, ms, 1)
        p8 = lambda a: pad_rows(a.reshape(bs, ts, a.shape[-1]))
        zs8 = pad_rows(zs3)
        seg8 = lambda name, w: zs8[:, :, _PACK_OFF[name]:_PACK_OFF[name] + w]
        new_t = lambda a: jnp.pad(jnp.swapaxes(a.reshape(bs, ts, a.shape[-1]), 1, 2),
                                  ((0, 0), (0, 0), (0, LANES - ts)))
        yb8 = _dsa_sample(page_table, p8(q), p8(qi), seg8("zwi", LANES), seg8("b_gate", D_B),
                          new_t(k), new_t(v), new_t(ki), ckt, cvt, ckit, l, depth, ts)
        yc8, s_s = _hgrn(zs8, lb, c_norm_g[l], state_hgrn[l], SUBLANES, ts)
        xs = _merge(xs, ya_s, yb8[:, :ts].reshape(ms, D_B), yc8[:, :ts].reshape(ms, D_C), zs,
                    per_row(gate), wa, wb, wc, wo, ms, 1)
        outs[5].append(k.reshape(bs, ts, N_KV_HEADS, HEAD_DIM))
        outs[6].append(v.reshape(bs, ts, N_KV_HEADS, HEAD_DIM))
        outs[7].append(ki.reshape(bs, ts, IDX_DIM))
        outs[8].append(jnp.concatenate([state_conv[l][:, ts:], jnp.swapaxes(u_t, 0, 1)], axis=1))
        outs[9].append(s_s)

    st = [jnp.stack(o) for o in outs]
    return (xp.reshape(bp, tp, d), xs.reshape(bs, ts, d), st[0], st[1], st[2], st[3], st[4],
            st[5], st[6], st[7], st[8], st[9])
```

```python
import functools
import math

import jax
import jax.numpy as jnp
import numpy as np
from jax import lax
from jax.experimental import pallas as pl
from jax.experimental.pallas import tpu as pltpu

F32 = jnp.float32
BF16 = jnp.bfloat16
I32 = jnp.int32

LANES = 128
SUBLANES = 8
VMEM_LIMIT = 56 * 1024 * 1024

D_MODEL = 1024
EPS = 1e-6
NEG = -1e30
LB_FLOOR = 1e-30
D_A = D_MODEL // 2
CONV_W = 31
HALO = 32
N_HEADS = 8
N_KV_HEADS = 2
HEAD_DIM = 64
D_B = N_HEADS * HEAD_DIM
KV_DIM = N_KV_HEADS * HEAD_DIM
IDX_HEADS = 4
IDX_DIM = 64
TOPK_MAX = 256
ROPE_THETA = 10000.0
Q_BLOCK = 128
C_HEADS = 4
C_KDIM = 128
C_VDIM = D_MODEL // 2 // C_HEADS
D_C = C_HEADS * C_VDIM
C_FDIM = C_HEADS * C_KDIM
CHUNK = 64
PAGE_SIZE = 128

_IN_NAMES = ("a_val", "a_glu", "a_gate", "zq", "zk", "zv", "zqi", "zki", "zwi", "b_gate",
             "cq", "cf", "ci", "c_gate", "g_a", "g_b", "g_c")
_IN_WIDTHS = (D_A, D_A, D_A, D_B, KV_DIM, KV_DIM, IDX_HEADS * IDX_DIM, IDX_DIM, IDX_HEADS, D_B,
              C_FDIM, C_FDIM, D_C, D_C, D_MODEL, D_MODEL, D_MODEL)
_IN_OFFS = dict(zip(_IN_NAMES, np.concatenate([[0], np.cumsum(_IN_WIDTHS)[:-1]]).tolist()))
_IN_W = dict(zip(_IN_NAMES, _IN_WIDTHS))
_PACK_ORDER = ("g_a", "g_b", "g_c", "a_val", "a_glu", "a_gate", "zq", "b_gate", "cq", "cf", "ci",
               "c_gate", "zqi", "zk", "zv", "zki", "zwi")


def _round_up(n, m):
    return (n + m - 1) // m * m


_PACK_W = {n: _round_up(_IN_W[n], LANES) for n in _PACK_ORDER}
_PACK_OFF = {}
_o = 0
for _n in _PACK_ORDER:
    assert _o % _PACK_W[_n] == 0
    _PACK_OFF[_n] = _o
    _o += _PACK_W[_n]
NZ = _o


def _pack_w_in_t(w):
    wt = jnp.swapaxes(w, 0, 1)
    parts = []
    for n in _PACK_ORDER:
        seg = wt[_IN_OFFS[n]:_IN_OFFS[n] + _IN_W[n]]
        pad = _PACK_W[n] - _IN_W[n]
        if pad:
            seg = jnp.pad(seg, ((0, pad), (0, 0)))
        parts.append(seg)
    return jnp.concatenate(parts, axis=0)


def _col(name, width):
    assert _PACK_OFF[name] % width == 0
    return _PACK_OFF[name] // width


def _sigmoid(x):
    return jax.nn.sigmoid(x)


def _silu(x):
    return x * jax.nn.sigmoid(x)


def _params(sem):
    return pltpu.CompilerParams(dimension_semantics=sem, vmem_limit_bytes=VMEM_LIMIT)


def _ada_kernel(c_ref, w_ref, b_ref, o_ref):
    c = c_ref[...]
    o_ref[...] = jnp.dot(_silu(c).astype(BF16), w_ref[...], preferred_element_type=F32) + b_ref[...]


def _ada(c, w_bf, b):
    n, d = c.shape
    nout = w_bf.shape[1]
    tn = D_MODEL
    return pl.pallas_call(
        _ada_kernel,
        grid=(nout // tn,),
        in_specs=[pl.BlockSpec((n, d), lambda j: (0, 0)),
                  pl.BlockSpec((d, tn), lambda j: (0, j)),
                  pl.BlockSpec((1, tn), lambda j: (0, j))],
        out_specs=pl.BlockSpec((n, tn), lambda j: (0, j)),
        out_shape=jax.ShapeDtypeStruct((n, nout), F32),
        compiler_params=_params(("arbitrary",)),
        name="ada",
    )(c, w_bf, b.reshape(1, nout))


def _prenorm_kernel(x_ref, g_ref, sc_ref, sh_ref, o_ref):
    x = x_ref[...]
    ms = jnp.mean(x * x, axis=-1, keepdims=True)
    y = x * lax.rsqrt(ms + EPS) * g_ref[...]
    o_ref[...] = (y * (1.0 + sc_ref[0]) + sh_ref[0]).astype(o_ref.dtype)


def _prenorm(x2, g, scale3, shift3, tm, tiles_per_mod):
    m, d = x2.shape
    r = scale3.shape[1]
    return pl.pallas_call(
        _prenorm_kernel,
        grid=(m // tm,),
        in_specs=[pl.BlockSpec((tm, d), lambda i: (i, 0)),
                  pl.BlockSpec((1, d), lambda i: (0, 0)),
                  pl.BlockSpec((1, r, d), lambda i: (i // tiles_per_mod, 0, 0)),
                  pl.BlockSpec((1, r, d), lambda i: (i // tiles_per_mod, 0, 0))],
        out_specs=pl.BlockSpec((tm, d), lambda i: (i, 0)),
        out_shape=jax.ShapeDtypeStruct((m, d), BF16),
        compiler_params=_params(("parallel",)),
        name="prenorm",
    )(x2, g.reshape(1, d), scale3, shift3)


def _mm_kernel(a_ref, bt_ref, o_ref):
    o_ref[...] = lax.dot_general(a_ref[...], bt_ref[...], (((1,), (1,)), ((), ())),
                                 preferred_element_type=F32)


def _matmul_nt(a_bf, bt_bf, tm, tn):
    m, k = a_bf.shape
    n = bt_bf.shape[0]
    return pl.pallas_call(
        _mm_kernel,
        grid=(n // tn, m // tm),
        in_specs=[pl.BlockSpec((tm, k), lambda j, i: (i, 0)),
                  pl.BlockSpec((tn, k), lambda j, i: (j, 0))],
        out_specs=pl.BlockSpec((tm, tn), lambda j, i: (i, j)),
        out_shape=jax.ShapeDtypeStruct((m, n), F32),
        compiler_params=_params(("parallel", "parallel")),
        name="inproj",
    )(a_bf, bt_bf)


def _ln_swish_gate(y, gate, bdw, lng, lnb):
    y = y + bdw
    mu = jnp.mean(y, axis=-1, keepdims=True)
    yc = y - mu
    var = jnp.mean(yc * yc, axis=-1, keepdims=True)
    yn = yc * lax.rsqrt(var + EPS) * lng + lnb
    return _silu(yn) * _silu(gate)


def _conv_prompt_kernel(val_ref, glu_ref, gate_ref, valh_ref, gluh_ref, buf_ref, w_ref, bdw_ref,
                        lng_ref, lnb_ref, ya_ref, utail_ref, f_ref, g_ref, *, tt):
    ti = pl.program_id(1)
    u = val_ref[0] * _sigmoid(glu_ref[0])
    uh = valh_ref[0] * _sigmoid(gluh_ref[0])
    f_ref[0:HALO, :] = jnp.where(ti == 0, buf_ref[0], uh)
    f_ref[HALO:HALO + tt, :] = u
    nrows = g_ref.shape[1]
    for p in range(1, SUBLANES):
        g_ref[p - 1] = f_ref[pl.ds(p, nrows), :]
    off = HALO - (CONV_W - 1)
    acc = jnp.zeros((tt, D_A), F32)
    for j in range(CONV_W):
        p = (off + j) % SUBLANES
        a = off + j - p
        win = f_ref[pl.ds(a, tt), :] if p == 0 else g_ref[p - 1, pl.ds(a, tt), :]
        acc = acc + w_ref[j:j + 1, :] * win
    ya_ref[0] = _ln_swish_gate(acc, gate_ref[0], bdw_ref[...], lng_ref[...], lnb_ref[...])

    @pl.when(ti == pl.num_programs(1) - 1)
    def _():
        utail_ref[0] = f_ref[tt:tt + HALO, :]


def _conv_prompt(z3, buf32, w_dw, b_dw, ln_g, ln_b, tt):
    b, t, _ = z3.shape
    hb = tt // HALO
    cur = lambda name: pl.BlockSpec((1, tt, D_A), lambda bi, ti, c=_col(name, D_A): (bi, ti, c))
    halo = lambda name: pl.BlockSpec(
        (1, HALO, D_A), lambda bi, ti, c=_col(name, D_A): (bi, jnp.maximum(ti * hb - 1, 0), c))
    vec = pl.BlockSpec((1, D_A), lambda bi, ti: (0, 0))
    wpad = jnp.pad(w_dw, ((0, HALO - CONV_W), (0, 0)))
    return pl.pallas_call(
        functools.partial(_conv_prompt_kernel, tt=tt),
        grid=(b, t // tt),
        in_specs=[cur("a_val"), cur("a_glu"), cur("a_gate"), halo("a_val"), halo("a_glu"),
                  pl.BlockSpec((1, HALO, D_A), lambda bi, ti: (bi, 0, 0)),
                  pl.BlockSpec((HALO, D_A), lambda bi, ti: (0, 0)), vec, vec, vec],
        out_specs=[pl.BlockSpec((1, tt, D_A), lambda bi, ti: (bi, ti, 0)),
                   pl.BlockSpec((1, HALO, D_A), lambda bi, ti: (bi, 0, 0))],
        out_shape=[jax.ShapeDtypeStruct((b, t, D_A), F32),
                   jax.ShapeDtypeStruct((b, HALO, D_A), F32)],
        scratch_shapes=[pltpu.VMEM((HALO + tt, D_A), F32),
                        pltpu.VMEM((SUBLANES - 1, HALO + tt - SUBLANES, D_A), F32)],
        compiler_params=_params(("parallel", "arbitrary")),
        name="conv_prompt",
    )(z3, z3, z3, z3, z3, buf32, wpad, b_dw.reshape(1, D_A), ln_g.reshape(1, D_A), ln_b.reshape(1, D_A))


def _conv_sample_kernel(val_ref, glu_ref, gate_ref, buf_ref, w_ref, bdw_ref, lng_ref, lnb_ref,
                        ya_ref, u_ref, *, ts):
    nb = CONV_W - 1
    for t in range(ts):
        u_ref[t] = val_ref[t] * _sigmoid(glu_ref[t])
    for t in range(ts):
        acc = jnp.zeros(u_ref.shape[1:], F32)
        for j in range(CONV_W):
            r = t + j
            src = buf_ref[r] if r < nb else u_ref[r - nb]
            acc = acc + w_ref[j:j + 1, :] * src
        ya_ref[t] = _ln_swish_gate(acc, gate_ref[t], bdw_ref[...], lng_ref[...], lnb_ref[...])


def _conv_sample(val_t, glu_t, gate_t, buf_t, w_dw, b_dw, ln_g, ln_b):
    ts, b, _ = val_t.shape
    assert ts <= CONV_W - 1
    return pl.pallas_call(
        functools.partial(_conv_sample_kernel, ts=ts),
        out_shape=[jax.ShapeDtypeStruct((ts, b, D_A), F32), jax.ShapeDtypeStruct((ts, b, D_A), F32)],
        compiler_params=pltpu.CompilerParams(vmem_limit_bytes=VMEM_LIMIT),
        name="conv_sample",
    )(val_t, glu_t, gate_t, buf_t, w_dw, b_dw.reshape(1, D_A), ln_g.reshape(1, D_A), ln_b.reshape(1, D_A))


def _lane_iota(shape):
    return lax.broadcasted_iota(I32, shape, len(shape) - 1)


def _swap_half(x):
    w = x.shape[-1]
    half = HEAD_DIM // 2
    first = (_lane_iota(x.shape) % HEAD_DIM) < half
    return jnp.where(first, pltpu.roll(x, w - half, 1), pltpu.roll(x, half, 1))


def _group_sum(s):
    same_group = (lax.broadcasted_iota(I32, (LANES, LANES), 0) // HEAD_DIM
                  == lax.broadcasted_iota(I32, (LANES, LANES), 1) // HEAD_DIM).astype(BF16)
    hi = s.astype(BF16)
    r1 = s - hi.astype(F32)
    mid = r1.astype(BF16)
    lo = (r1 - mid.astype(F32)).astype(BF16)
    cols = []
    for c in range(s.shape[-1] // LANES):
        sl = slice(c * LANES, (c + 1) * LANES)
        cols.append(jnp.dot(hi[:, sl], same_group, preferred_element_type=F32)
                    + jnp.dot(mid[:, sl], same_group, preferred_element_type=F32)
                    + jnp.dot(lo[:, sl], same_group, preferred_element_type=F32))
    return cols[0] if len(cols) == 1 else jnp.concatenate(cols, axis=-1)


def _tile_lanes(t, w):
    reps = w // t.shape[-1]
    return t if reps == 1 else jnp.concatenate([t] * reps, axis=-1)


def _rope(x, cos, sin):
    w = x.shape[-1]
    return x * _tile_lanes(cos, w) + _swap_half(x) * _tile_lanes(sin, w)


def _head_rms(x, g):
    ms = _group_sum(x * x) * (1.0 / HEAD_DIM)
    y = x * lax.rsqrt(ms + EPS)
    return y if g is None else y * _tile_lanes(g, x.shape[-1])


def _qkrope_kernel(zq_ref, zk_ref, zqi_ref, zki_ref, zv_ref, cos_ref, sin_ref, qg_ref, kg_ref,
                   q_ref, k_ref, qi_ref, ki_ref, v_ref, kb_ref, vb_ref, kib_ref, *, feature_major):
    cos = cos_ref[...]
    sin = sin_ref[...]
    q_ref[...] = _rope(_head_rms(zq_ref[...], qg_ref[...]), cos, sin)
    k = _rope(_head_rms(zk_ref[...], kg_ref[...]), cos, sin)
    kb_ref[...] = k.astype(BF16)
    qi_ref[...] = _rope(zqi_ref[...], cos, sin)
    ki = _rope(_head_rms(zki_ref[...], None), cos, sin)
    kib_ref[...] = ki[:, :IDX_DIM].astype(BF16)
    v = zv_ref[...]
    vb_ref[...] = v.astype(BF16)
    if feature_major:
        k_ref[0] = k.T
        ki_ref[0] = ki.T[:IDX_DIM]
        v_ref[0] = v.T
    else:
        k_ref[...] = k
        ki_ref[...] = ki[:, :IDX_DIM]
        v_ref[...] = v


def _qkrope(z, cos_t, sin_t, qg, kg, tm, table_tiles, feature_major):
    m = z.shape[0]
    zc = lambda name, w: pl.BlockSpec((tm, w), lambda i, c=_col(name, w): (i, c))
    tab = pl.BlockSpec((tm, LANES), lambda i: (i % table_tiles, 0))
    vec = pl.BlockSpec((1, LANES), lambda i: (0, 0))
    row = lambda w: pl.BlockSpec((tm, w), lambda i: (i, 0))
    qi_w = IDX_HEADS * IDX_DIM
    if feature_major:
        nseq, t = m // (tm * table_tiles), tm * table_tiles
        fm = lambda w: pl.BlockSpec((1, w, tm), lambda i: (i // table_tiles, 0, i % table_tiles))
        f32_spec = lambda w: fm(w)
        f32_shape = lambda w: jax.ShapeDtypeStruct((nseq, w, t), F32)
    else:
        f32_spec = row
        f32_shape = lambda w: jax.ShapeDtypeStruct((m, w), F32)
    return pl.pallas_call(
        functools.partial(_qkrope_kernel, feature_major=feature_major),
        grid=(m // tm,),
        in_specs=[zc("zq", D_B), zc("zk", KV_DIM), zc("zqi", qi_w), zc("zki", LANES), zc("zv", KV_DIM),
                  tab, tab, vec, vec],
        out_specs=[row(D_B), f32_spec(KV_DIM), row(qi_w), f32_spec(IDX_DIM), f32_spec(KV_DIM),
                   row(KV_DIM), row(KV_DIM), row(IDX_DIM)],
        out_shape=[jax.ShapeDtypeStruct((m, D_B), F32), f32_shape(KV_DIM),
                   jax.ShapeDtypeStruct((m, qi_w), F32), f32_shape(IDX_DIM), f32_shape(KV_DIM),
                   jax.ShapeDtypeStruct((m, KV_DIM), BF16), jax.ShapeDtypeStruct((m, KV_DIM), BF16),
                   jax.ShapeDtypeStruct((m, IDX_DIM), BF16)],
        compiler_params=_params(("parallel",)),
        name="qkrope",
    )(z, z, z, z, z, cos_t, sin_t, qg, kg)


def _rope_tables(pos):
    half = HEAD_DIM // 2
    inv = ROPE_THETA ** (-jnp.arange(half, dtype=F32) / half)
    ang = pos.astype(F32)[:, None] * inv[None, :]
    cos = jnp.cos(ang)
    sin = jnp.sin(ang)
    cos64 = jnp.concatenate([cos, cos], axis=1)
    sin64 = jnp.concatenate([-sin, sin], axis=1)
    reps = LANES // HEAD_DIM
    return jnp.tile(cos64, (1, reps)), jnp.tile(sin64, (1, reps))


_SIGN = np.int32(-2 ** 31)
_MAG = np.int32(0x7FFFFFFF)
QK_SCALE = HEAD_DIM ** -0.5 * math.log2(math.e)
SEARCH_UNROLL = 4
SAMPLE_DIGIT_BITS = 3


def _dot_nt(a, b):
    return lax.dot_general(a, b, (((1,), (1,)), ((), ())), preferred_element_type=F32)


def _row_count(mask):
    return jnp.sum(mask.astype(F32), axis=1, keepdims=True)


def _ordered_to_float(t):
    key = t ^ _SIGN
    return lax.bitcast_convert_type(jnp.where(key < 0, key ^ _MAG, key), F32)


def _threshold_bits(sc_ref, kf):
    nq, nl = sc_ref.shape

    def cond(c):
        i, _, cnt_t = c
        return (i < 32) & (jnp.max(jnp.abs(cnt_t - kf)) > 0.0)

    def body(c):
        i, t, cnt_t = c
        for j in range(SEARCH_UNROLL):
            cand = t | jnp.left_shift(jnp.int32(1), 31 - (i + j))
            cnt = _row_count(sc_ref[...] >= _ordered_to_float(cand))
            ok = cnt >= kf
            t = jnp.where(ok, cand, t)
            cnt_t = jnp.where(ok, cnt, cnt_t)
        return i + SEARCH_UNROLL, t, cnt_t

    init = (jnp.int32(0), jnp.zeros((nq, 1), I32), jnp.full((nq, 1), float(nl), F32))
    return lax.while_loop(cond, body, init)[1]


def _threshold_digits(sc_ref, kf, digit_bits):
    nq, _ = sc_ref.shape
    t = jnp.zeros((nq, 1), I32)
    pos = 32
    while pos > 0:
        nb = (pos % digit_bits) or digit_bits
        pos -= nb
        digit = jnp.zeros((nq, 1), I32)
        for j in range(1, 2 ** nb):
            cand = t | np.uint32(j << pos).astype(np.int32)
            ok = _row_count(sc_ref[...] >= _ordered_to_float(cand)) >= kf
            digit = digit + ok.astype(I32)
        t = t | jnp.left_shift(digit, pos)
    return t


def _select_topk(sc_ref, sel_ref, topk, digit_bits=1):
    nq, nl = sc_ref.shape
    if nl == topk:
        sel_ref[...] = jnp.ones((nq, nl), F32)
        return
    kf = float(topk)
    t = _threshold_bits(sc_ref, kf) if digit_bits == 1 else _threshold_digits(sc_ref, kf, digit_bits)
    tau = jnp.where(t == 0, -jnp.inf, _ordered_to_float(t))
    sc = sc_ref[...]
    gt = sc > tau
    eq = sc == tau
    need = kf - _row_count(gt)
    sel_ref[...] = (gt | eq).astype(F32)
    tie_rows = (_row_count(eq) > need) & (tau > NEG)

    @pl.when(jnp.max(tie_rows.astype(F32)) > 0.0)
    def _():
        tri = (lax.broadcasted_iota(I32, (LANES, LANES), 0)
               < lax.broadcasted_iota(I32, (LANES, LANES), 1)).astype(BF16)
        run = jnp.zeros((nq, 1), F32)
        for j in range(nl // LANES):
            sl = slice(j * LANES, (j + 1) * LANES)
            sj = sc_ref[:, sl]
            eqj = sj == tau
            rank = jnp.dot(eqj.astype(BF16), tri, preferred_element_type=F32) + run
            sel_ref[:, sl] = ((sj > tau) | (eqj & (rank < need))).astype(F32)
            run = run + _row_count(eqj)


def _dsa_core(q, qi, wi, bgate, qpos, kb_ref, vb_ref, kib_ref, sc_ref, sel_ref, topk):
    nq = q.shape[0]
    nl = kb_ref.shape[0]
    kib = kib_ref[...]
    qis = (qi * (IDX_DIM ** -0.5)).astype(BF16)
    wis = wi * (IDX_HEADS ** -0.5)
    scores = jnp.zeros((nq, nl), F32)
    for h in range(IDX_HEADS):
        s = _dot_nt(qis[:, h * IDX_DIM:(h + 1) * IDX_DIM], kib)
        scores = scores + jnp.maximum(s, 0.0) * wis[:, h:h + 1]
    causal = _lane_iota((nq, nl)) <= qpos
    sc_ref[...] = jnp.where(causal, scores, NEG)
    _select_topk(sc_ref, sel_ref, topk)

    sel = (sel_ref[...] > 0.0) & causal
    lane = _lane_iota((nq, LANES))
    group_w = N_HEADS // N_KV_HEADS
    kb = kb_ref[...]
    vb = vb_ref[...]
    qs = q * QK_SCALE
    outs = []
    for h in range(N_HEADS):
        g = h // group_w
        c = (h * HEAD_DIM) // LANES
        x = qs[:, c * LANES:(c + 1) * LANES]
        if (h % 2) != g:
            x = pltpu.roll(x, HEAD_DIM, 1)
        in_g = (lane // HEAD_DIM) == g
        xq = jnp.where(in_g, x, 0.0).astype(BF16)
        s = jnp.where(sel, _dot_nt(xq, kb), NEG)
        mx = jnp.max(s, axis=1, keepdims=True)
        p = jnp.exp2(s - mx)
        den = jnp.sum(p, axis=1, keepdims=True)
        o = jnp.dot(p.astype(BF16), vb, preferred_element_type=F32) / den
        o = jnp.where(in_g, o, 0.0)
        if (h % 2) != g:
            o = pltpu.roll(o, HEAD_DIM, 1)
        outs.append(o)
    cols = [outs[2 * c] + outs[2 * c + 1] for c in range(N_HEADS // 2)]
    return jnp.concatenate(cols, axis=1) * _silu(bgate)


def _dsa_prompt_kernel(q_ref, qi_ref, wi_ref, bg_ref, kb_ref, vb_ref, kib_ref, o_ref, sc_ref, sel_ref,
                       *, topk, q_lo):
    qb = pl.program_id(1) + q_lo
    nq = q_ref.shape[1]
    qpos = qb * nq + lax.broadcasted_iota(I32, (nq, 1), 0)
    o_ref[0] = _dsa_core(q_ref[0], qi_ref[0], wi_ref[0], bg_ref[0], qpos,
                         kb_ref.at[0], vb_ref.at[0], kib_ref.at[0], sc_ref, sel_ref, topk)


def _dsa_prompt_bucket(q3, qi3, z3, kb3, vb3, kib3, nq, q_lo, q_hi):
    b, t, _ = q3.shape
    topk = min(TOPK_MAX, t // 4)
    nl = q_hi * nq
    qi_w = IDX_HEADS * IDX_DIM
    qblk = lambda w: pl.BlockSpec((1, nq, w), lambda bi, i: (bi, i + q_lo, 0))
    zblk = lambda name, w: pl.BlockSpec((1, nq, w), lambda bi, i, c=_col(name, w): (bi, i + q_lo, c))
    keys = lambda w: pl.BlockSpec((1, nl, w), lambda bi, i: (bi, 0, 0))
    return pl.pallas_call(
        functools.partial(_dsa_prompt_kernel, topk=topk, q_lo=q_lo),
        grid=(b, q_hi - q_lo),
        in_specs=[qblk(D_B), qblk(qi_w), zblk("zwi", LANES), zblk("b_gate", D_B),
                  keys(KV_DIM), keys(KV_DIM), keys(IDX_DIM)],
        out_specs=pl.BlockSpec((1, nq, D_B), lambda bi, i: (bi, i, 0)),
        out_shape=jax.ShapeDtypeStruct((b, (q_hi - q_lo) * nq, D_B), F32),
        scratch_shapes=[pltpu.VMEM((nq, nl), F32), pltpu.VMEM((nq, nl), F32)],
        compiler_params=_params(("parallel", "parallel")),
        name="dsa_prompt",
    )(q3, qi3, z3, z3, kb3, vb3, kib3)


DSA_BUCKETS = 8
DSA_ROWS = 256


def _dsa_prompt(q3, qi3, z3, kb3, vb3, kib3):
    t = q3.shape[1]
    nq = DSA_ROWS if t % DSA_ROWS == 0 else Q_BLOCK
    nblk = t // nq
    step = max(1, nblk // DSA_BUCKETS)
    parts = [_dsa_prompt_bucket(q3, qi3, z3, kb3, vb3, kib3, nq, lo, min(lo + step, nblk))
             for lo in range(0, nblk, step)]
    return parts[0] if len(parts) == 1 else jnp.concatenate(parts, axis=1)


def _dsa_sample_kernel(pt_ref, q_ref, qi_ref, wi_ref, bg_ref, knt_ref, vnt_ref, kint_ref,
                       ck_hbm, cv_hbm, cki_hbm, o_ref, kt_ref, vt_ref, kit_ref, sc_ref, sel_ref, sems,
                       *, topk, n_pages, page_base, ts):
    bi = pl.program_id(0)
    nb = pl.num_programs(0)
    nq = q_ref.shape[1]
    past = n_pages * PAGE_SIZE
    nl = kt_ref.shape[2]
    group_w = N_HEADS // N_KV_HEADS
    slot = bi % 2

    def copies(seq, s, p):
        page = pt_ref[seq, p] + page_base
        dst = pl.ds(p * PAGE_SIZE, PAGE_SIZE)
        return (pltpu.make_async_copy(ck_hbm.at[page], kt_ref.at[s, :, dst], sems.at[s, 0]),
                pltpu.make_async_copy(cv_hbm.at[page], vt_ref.at[s, :, dst], sems.at[s, 1]),
                pltpu.make_async_copy(cki_hbm.at[page], kit_ref.at[s, :, dst], sems.at[s, 2]))

    def start_gather(seq, s):
        for p in range(n_pages):
            for cp in copies(seq, s, p):
                cp.start()

    @pl.when(bi == 0)
    def _():
        start_gather(0, 0)

    @pl.when(bi + 1 < nb)
    def _():
        start_gather(bi + 1, 1 - slot)

    kt_ref[slot, :, past:nl] = knt_ref[0]
    vt_ref[slot, :, past:nl] = vnt_ref[0]
    kit_ref[slot, :, past:nl] = kint_ref[0]

    for p in range(n_pages):
        for cp in copies(bi, slot, p):
            cp.wait()
    kt_ref, vt_ref, kit_ref = kt_ref.at[slot], vt_ref.at[slot], kit_ref.at[slot]

    row = lax.broadcasted_iota(I32, (nq, 1), 0)
    qpos = past + jnp.minimum(row, ts - 1)
    stack = lambda x, w, heads: jnp.concatenate([x[:, h * w:(h + 1) * w] for h in heads], axis=0)

    qis = stack(qi_ref[0] * (IDX_DIM ** -0.5), IDX_DIM, range(IDX_HEADS)).astype(BF16)
    s = jnp.dot(qis, kit_ref[...].astype(BF16), preferred_element_type=F32)
    wis = wi_ref[0] * (IDX_HEADS ** -0.5)
    scores = jnp.zeros((nq, nl), F32)
    for h in range(IDX_HEADS):
        scores = scores + jnp.maximum(s[h * nq:(h + 1) * nq], 0.0) * wis[:, h:h + 1]
    causal = _lane_iota((nq, nl)) <= qpos
    sc_ref[...] = jnp.where(causal, scores, NEG)
    _select_topk(sc_ref, sel_ref, topk, digit_bits=SAMPLE_DIGIT_BITS)

    sel = (sel_ref[...] > 0.0) & causal
    sel_g = jnp.concatenate([sel] * group_w, axis=0)
    qs = q_ref[0] * QK_SCALE
    outs = []
    for g in range(N_KV_HEADS):
        rows = slice(g * HEAD_DIM, (g + 1) * HEAD_DIM)
        qg = stack(qs, HEAD_DIM, range(g * group_w, (g + 1) * group_w)).astype(BF16)
        sg = jnp.dot(qg, kt_ref[rows, :].astype(BF16), preferred_element_type=F32)
        sg = jnp.where(sel_g, sg, NEG)
        mx = jnp.max(sg, axis=1, keepdims=True)
        p = jnp.exp2(sg - mx)
        den = jnp.sum(p, axis=1, keepdims=True)
        og = _dot_nt(p.astype(BF16), vt_ref[rows, :].astype(BF16)) / den
        outs += [og[j * nq:(j + 1) * nq] for j in range(group_w)]
    o_ref[0] = jnp.concatenate(outs, axis=1) * _silu(bg_ref[0])


def _dsa_sample(page_table, q8, qi8, wi8, bg8, knt, vnt, kint, ckt, cvt, ckit, layer, depth, ts):
    b, nq, _ = q8.shape
    n_pages = page_table.shape[1]
    past = n_pages * PAGE_SIZE
    nl = past + LANES
    topk = min(TOPK_MAX, (past + ts) // 4)
    n_pool = ckt.shape[0] // depth
    qi_w = IDX_HEADS * IDX_DIM
    blk = lambda r, w: pl.BlockSpec((1, r, w), lambda bi, pt: (bi, 0, 0))
    anyspec = pl.BlockSpec(memory_space=pl.ANY)
    grid_spec = pltpu.PrefetchScalarGridSpec(
        num_scalar_prefetch=1,
        grid=(b,),
        in_specs=[blk(nq, D_B), blk(nq, qi_w), blk(nq, LANES), blk(nq, D_B),
                  blk(KV_DIM, LANES), blk(KV_DIM, LANES), blk(IDX_DIM, LANES),
                  anyspec, anyspec, anyspec],
        out_specs=pl.BlockSpec((1, nq, D_B), lambda bi, pt: (bi, 0, 0)),
        scratch_shapes=[pltpu.VMEM((2, KV_DIM, nl), F32), pltpu.VMEM((2, KV_DIM, nl), F32),
                        pltpu.VMEM((2, IDX_DIM, nl), F32),
                        pltpu.VMEM((nq, nl), F32), pltpu.VMEM((nq, nl), F32),
                        pltpu.SemaphoreType.DMA((2, 3))])
    return pl.pallas_call(
        functools.partial(_dsa_sample_kernel, topk=topk, n_pages=n_pages, page_base=layer * n_pool, ts=ts),
        grid_spec=grid_spec,
        out_shape=jax.ShapeDtypeStruct((b, nq, D_B), F32),
        compiler_params=_params(("arbitrary",)),
        name="dsa_sample",
    )(page_table, q8, qi8, wi8, bg8, knt, vnt, kint, ckt, cvt, ckit)


def _hgrn_kernel(cq_ref, cf_ref, ci_ref, cg_ref, lb_ref, ng_ref, s0_ref, yc_ref, s_ref, *, c, valid):
    ci_idx = pl.program_id(1)

    @pl.when(ci_idx == 0)
    def _():
        s_ref[...] = s0_ref[...]

    for r in range(cq_ref.shape[0]):
        _hgrn_chunk(r, cq_ref, cf_ref, ci_ref, cg_ref, lb_ref, ng_ref, yc_ref, s_ref, c, valid)


def _hgrn_chunk(r, cq_ref, cf_ref, ci_ref, cg_ref, lb_ref, ng_ref, yc_ref, s_ref, c, valid):
    fx = cf_ref[r]
    lb = lb_ref[...]
    log_f = jnp.log(jnp.maximum(lb, LB_FLOOR) + (1.0 - lb) * _sigmoid(fx))
    kk = (1.0 - lb) * _sigmoid(-fx)
    qc = _silu(cq_ref[r])
    iv = ci_ref[r]
    row = lax.broadcasted_iota(I32, (c, 1), 0)
    if valid < c:
        log_f = jnp.where(row < valid, log_f, 0.0)

    rr = lax.broadcasted_iota(I32, (c, c), 0)
    cc = lax.broadcasted_iota(I32, (c, c), 1)
    hs = [slice(h * C_KDIM, (h + 1) * C_KDIM) for h in range(C_HEADS)]
    att = [jnp.zeros((c, c), F32) for _ in range(C_HEADS)]
    cs = log_f
    tot = log_f
    m = 1
    while m < c:
        right = ((row // m) % 2) == 1
        qm = jnp.where(right, qc * jnp.exp(cs), 0.0).astype(BF16)
        km = jnp.where(right, 0.0, kk * jnp.exp(tot - cs)).astype(BF16)
        pair = (rr // (2 * m)) == (cc // (2 * m))
        for h in range(C_HEADS):
            att[h] = att[h] + jnp.where(pair, _dot_nt(qm[:, hs[h]], km[:, hs[h]]), 0.0)
        sib = jnp.where(right, pltpu.roll(tot, m, 0), pltpu.roll(tot, c - m, 0))
        cs = jnp.where(right, cs + sib, cs)
        tot = tot + sib
        m *= 2
    qdec = (qc * jnp.exp(cs)).astype(BF16)
    kdec = (kk * jnp.exp(tot - cs)).astype(BF16)
    ivb = iv.astype(BF16)
    eye_c = rr == cc
    eye_k = (lax.broadcasted_iota(I32, (C_KDIM, C_KDIM), 0)
             == lax.broadcasted_iota(I32, (C_KDIM, C_KDIM), 1))
    ys = []
    for h in range(C_HEADS):
        sl = hs[h]
        diag = jnp.sum(qc[:, sl] * kk[:, sl], axis=1, keepdims=True)
        a_h = att[h] + jnp.where(eye_c, diag, 0.0)
        s_h = s_ref[r, h]
        o = (jnp.dot(a_h.astype(BF16), ivb[:, sl], preferred_element_type=F32)
             + jnp.dot(qdec[:, sl], s_h.astype(BF16), preferred_element_type=F32))
        e_end = jnp.exp(tot[0:1, sl])
        e_col = jnp.sum(jnp.where(eye_k, e_end, 0.0), axis=1, keepdims=True)
        upd = lax.dot_general(kdec[:, sl], ivb[:, sl], (((0,), (0,)), ((), ())),
                              preferred_element_type=F32)
        s_ref[r, h] = e_col * s_h + upd
        ms = jnp.mean(o * o, axis=-1, keepdims=True)
        ys.append(o * lax.rsqrt(ms + EPS) * ng_ref[...])
    yc_ref[r] = jnp.concatenate(ys, axis=1) * _silu(cg_ref[r])


HGRN_SEQS = 4


def _hgrn(z3, lb, cng, s0, c, valid):
    b, t, _ = z3.shape
    nb = math.gcd(b, HGRN_SEQS)
    zblk = lambda name: pl.BlockSpec((nb, c, D_C), lambda bi, i, col=_col(name, D_C): (bi, i, col))
    sblk = pl.BlockSpec((nb, C_HEADS, C_KDIM, C_VDIM), lambda bi, i: (bi, 0, 0, 0))
    return pl.pallas_call(
        functools.partial(_hgrn_kernel, c=c, valid=valid),
        grid=(b // nb, t // c),
        in_specs=[zblk("cq"), zblk("cf"), zblk("ci"), zblk("c_gate"),
                  pl.BlockSpec((1, C_FDIM), lambda bi, i: (0, 0)),
                  pl.BlockSpec((1, C_VDIM), lambda bi, i: (0, 0)), sblk],
        out_specs=[pl.BlockSpec((nb, c, D_C), lambda bi, i: (bi, i, 0)), sblk],
        out_shape=[jax.ShapeDtypeStruct((b, t, D_C), F32),
                   jax.ShapeDtypeStruct((b, C_HEADS, C_KDIM, C_VDIM), F32)],
        compiler_params=_params(("parallel", "arbitrary")),
        name="hgrn",
    )(z3, z3, z3, z3, lb.reshape(1, C_FDIM), cng.reshape(1, C_VDIM), s0)


def _merge_kernel(x_ref, ya_ref, yb_ref, yc_ref, ga_ref, gb_ref, gc_ref, gate_ref,
                  wa_ref, wb_ref, wc_ref, wo_ref, o_ref):
    def proj(y_ref, w_ref):
        return jnp.dot(y_ref[...].astype(BF16), w_ref[...], preferred_element_type=F32)

    m = (_sigmoid(ga_ref[...]) * proj(ya_ref, wa_ref)
         + _sigmoid(gb_ref[...]) * proj(yb_ref, wb_ref)
         + _sigmoid(gc_ref[...]) * proj(yc_ref, wc_ref))
    o_ref[...] = x_ref[...] + gate_ref[0] * jnp.dot(m.astype(BF16), wo_ref[...], preferred_element_type=F32)


def _merge(x2, ya, yb, yc, z, gate3, wa, wb, wc, wo, tm, tiles_per_mod):
    m, d = x2.shape
    r = gate3.shape[1]
    row = lambda w: pl.BlockSpec((tm, w), lambda i: (i, 0))
    zc = lambda name: pl.BlockSpec((tm, d), lambda i, c=_col(name, d): (i, c))
    wspec = lambda k: pl.BlockSpec((k, d), lambda i: (0, 0))
    return pl.pallas_call(
        _merge_kernel,
        grid=(m // tm,),
        in_specs=[row(d), row(D_A), row(D_B), row(D_C), zc("g_a"), zc("g_b"), zc("g_c"),
                  pl.BlockSpec((1, r, d), lambda i: (i // tiles_per_mod, 0, 0)),
                  wspec(D_A), wspec(D_B), wspec(D_C), wspec(d)],
        out_specs=row(d),
        out_shape=jax.ShapeDtypeStruct((m, d), F32),
        compiler_params=_params(("parallel",)),
        name="merge",
    )(x2, ya, yb, yc, z, z, z, gate3, wa, wb, wc, wo)


def _pick_tile(n, pref):
    t = min(pref, n)
    while n % t:
        t //= 2
    return t


def kernel(x_prompt, x_sample, cache_k, cache_v, cache_idx_k, state_conv, state_hgrn, page_table,
           c_prompt, c_sample, w_ada, b_ada, norm_g, w_in, w_dw, b_dw, ln_g, ln_b, q_norm_g, k_norm_g,
           lb_logits, c_norm_g, w_proj_a, w_proj_b, w_proj_c, w_out):
    depth = w_in.shape[0]
    bp, tp, d = x_prompt.shape
    bs, ts, _ = x_sample.shape
    mp, ms = bp * tp, bs * ts
    n_pages = page_table.shape[1]
    past = n_pages * PAGE_SIZE
    assert d == D_MODEL and tp % Q_BLOCK == 0 and tp % CHUNK == 0 and tp >= HALO
    assert ts <= SUBLANES and ms % SUBLANES == 0

    lbp = jax.nn.softmax(lb_logits.astype(F32), axis=0)
    lb_all = jnp.cumsum(lbp, axis=0) - lbp[0:1]
    cos_p, sin_p = _rope_tables(jnp.arange(tp))
    cos_s, sin_s = _rope_tables(past + (jnp.arange(ms) % ts))
    ckt = jnp.transpose(cache_k, (0, 1, 3, 4, 2)).reshape(-1, KV_DIM, PAGE_SIZE)
    cvt = jnp.transpose(cache_v, (0, 1, 3, 4, 2)).reshape(-1, KV_DIM, PAGE_SIZE)
    ckit = jnp.transpose(cache_idx_k, (0, 1, 3, 2)).reshape(-1, IDX_DIM, PAGE_SIZE)
    c_all = jnp.concatenate([c_prompt, c_sample], axis=0)

    tm_p = _pick_tile(mp, 1024)
    tm_p = _pick_tile(tp, tm_p)
    tt = _pick_tile(tp, 256)
    tm_g = _pick_tile(tp, 256)
    zeros_buf = jnp.zeros((bp, HALO, D_A), F32)
    s00 = jnp.zeros((bp, C_HEADS, C_KDIM, C_VDIM), F32)
    pad_rows = lambda a3: jnp.pad(a3, ((0, 0), (0, SUBLANES - ts), (0, 0)))

    xp = x_prompt.reshape(mp, d)
    xs = x_sample.reshape(ms, d)
    outs = [[] for _ in range(10)]
    for l in range(depth):
        mod = _ada(c_all, w_ada[l].astype(BF16), b_ada[l])
        shift, scale, gate = mod[:, :d], mod[:, d:2 * d], mod[:, 2 * d:]
        w_in_t = _pack_w_in_t(w_in[l]).astype(BF16)
        wa, wb, wc, wo = (w_proj_a[l].astype(BF16), w_proj_b[l].astype(BF16),
                          w_proj_c[l].astype(BF16), w_out[l].astype(BF16))
        qg = jnp.tile(q_norm_g[l], LANES // HEAD_DIM).reshape(1, LANES)
        kg = jnp.tile(k_norm_g[l], LANES // HEAD_DIM).reshape(1, LANES)
        lb = lb_all[l]

        per_seq = lambda a: a[:bp].reshape(bp, 1, d)
        h = _prenorm(xp, norm_g[l], per_seq(scale), per_seq(shift), tm_p, tp // tm_p)
        z = _matmul_nt(h, w_in_t, tm_p, NZ // 6)
        z3 = z.reshape(bp, tp, NZ)
        ya, utail = _conv_prompt(z3, zeros_buf, w_dw[l], b_dw[l], ln_g[l], ln_b[l], tt)
        q, kt, qi, kit, vt, kb, vb, kib = _qkrope(z, cos_p, sin_p, qg, kg, tm_p, tp // tm_p, True)
        r3 = lambda a: a.reshape(bp, tp, a.shape[-1])
        yb = _dsa_prompt(r3(q), r3(qi), z3, r3(kb), r3(vb), r3(kib))
        yc, s_p = _hgrn(z3, lb, c_norm_g[l], s00, CHUNK, CHUNK)
        xp = _merge(xp, ya.reshape(mp, D_A), yb.reshape(mp, D_B), yc.reshape(mp, D_C), z,
                    per_seq(gate), wa, wb, wc, wo, tm_g, tp // tm_g)
        heads_last = lambda a: jnp.transpose(a.reshape(bp, N_KV_HEADS, HEAD_DIM, tp), (0, 3, 1, 2))
        outs[0].append(heads_last(kt))
        outs[1].append(heads_last(vt))
        outs[2].append(jnp.swapaxes(kit, 1, 2))
        outs[3].append(utail[:, HALO - (CONV_W - 1):, :])
        outs[4].append(s_p)

        per_row = lambda a: jnp.repeat(a[bp:], ts, axis=0).reshape(1, ms, d)
        hs = _prenorm(xs, norm_g[l], per_row(scale), per_row(shift), ms, 1)
        zs = _matmul_nt(hs, w_in_t, ms, NZ // 6)
        zs3 = zs.reshape(bs, ts, NZ)
        seg_t = lambda name: jnp.swapaxes(
            zs3[:, :, _PACK_OFF[name]:_PACK_OFF[name] + D_A], 0, 1)
        ya_t, u_t = _conv_sample(seg_t("a_val"), seg_t("a_glu"), seg_t("a_gate"),
                                 jnp.swapaxes(state_conv[l], 0, 1), w_dw[l], b_dw[l], ln_g[l], ln_b[l])
        ya_s = jnp.swapaxes(ya_t, 0, 1).reshape(ms, D_A)
        q, k, qi, ki, v, _, _, _ = _qkrope(zs, cos_s, sin_s, qg, kg, ms, 1, False)
        p8 = lambda a: pad_rows(a.reshape(bs, ts, a.shape[-1]))
        zs8 = pad_rows(zs3)
        seg8 = lambda name, w: zs8[:, :, _PACK_OFF[name]:_PACK_OFF[name] + w]
        new_t = lambda a: jnp.pad(jnp.swapaxes(a.reshape(bs, ts, a.shape[-1]), 1, 2),
                                  ((0, 0), (0, 0), (0, LANES - ts)))
        yb8 = _dsa_sample(page_table, p8(q), p8(qi), seg8("zwi", LANES), seg8("b_gate", D_B),
                          new_t(k), new_t(v), new_t(ki), ckt, cvt, ckit, l, depth, ts)
        yc8, s_s = _hgrn(zs8, lb, c_norm_g[l], state_hgrn[l], SUBLANES, ts)
        xs = _merge(xs, ya_s, yb8[:, :ts].reshape(ms, D_B), yc8[:, :ts].reshape(ms, D_C), zs,
                    per_row(gate), wa, wb, wc, wo, ms, 1)
        outs[5].append(k.reshape(bs, ts, N_KV_HEADS, HEAD_DIM))
        outs[6].append(v.reshape(bs, ts, N_KV_HEADS, HEAD_DIM))
        outs[7].append(ki.reshape(bs, ts, IDX_DIM))
        outs[8].append(jnp.concatenate([state_conv[l][:, ts:], jnp.swapaxes(u_t, 0, 1)], axis=1))
        outs[9].append(s_s)

    st = [jnp.stack(o) for o in outs]
    return (xp.reshape(bp, tp, d), xs.reshape(bs, ts, d), st[0], st[1], st[2], st[3], st[4],
            st[5], st[6], st[7], st[8], st[9])
```

```python
import functools
import math

import jax
import jax.numpy as jnp
import numpy as np
from jax import lax
from jax.experimental import pallas as pl
from jax.experimental.pallas import tpu as pltpu

F32 = jnp.float32
BF16 = jnp.bfloat16
I32 = jnp.int32

LANES = 128
SUBLANES = 8
VMEM_LIMIT = 56 * 1024 * 1024

D_MODEL = 1024
EPS = 1e-6
NEG = -1e30
LB_FLOOR = 1e-30
D_A = D_MODEL // 2
CONV_W = 31
HALO = 32
N_HEADS = 8
N_KV_HEADS = 2
HEAD_DIM = 64
D_B = N_HEADS * HEAD_DIM
KV_DIM = N_KV_HEADS * HEAD_DIM
IDX_HEADS = 4
IDX_DIM = 64
TOPK_MAX = 256
ROPE_THETA = 10000.0
Q_BLOCK = 128
C_HEADS = 4
C_KDIM = 128
C_VDIM = D_MODEL // 2 // C_HEADS
D_C = C_HEADS * C_VDIM
C_FDIM = C_HEADS * C_KDIM
CHUNK = 64
PAGE_SIZE = 128

_IN_NAMES = ("a_val", "a_glu", "a_gate", "zq", "zk", "zv", "zqi", "zki", "zwi", "b_gate",
             "cq", "cf", "ci", "c_gate", "g_a", "g_b", "g_c")
_IN_WIDTHS = (D_A, D_A, D_A, D_B, KV_DIM, KV_DIM, IDX_HEADS * IDX_DIM, IDX_DIM, IDX_HEADS, D_B,
              C_FDIM, C_FDIM, D_C, D_C, D_MODEL, D_MODEL, D_MODEL)
_IN_OFFS = dict(zip(_IN_NAMES, np.concatenate([[0], np.cumsum(_IN_WIDTHS)[:-1]]).tolist()))
_IN_W = dict(zip(_IN_NAMES, _IN_WIDTHS))
_PACK_ORDER = ("g_a", "g_b", "g_c", "a_val", "a_glu", "a_gate", "zq", "b_gate", "cq", "cf", "ci",
               "c_gate", "zqi", "zk", "zv", "zki", "zwi")


def _round_up(n, m):
    return (n + m - 1) // m * m


_PACK_W = {n: _round_up(_IN_W[n], LANES) for n in _PACK_ORDER}
_PACK_OFF = {}
_o = 0
for _n in _PACK_ORDER:
    assert _o % _PACK_W[_n] == 0
    _PACK_OFF[_n] = _o
    _o += _PACK_W[_n]
NZ = _o


def _pack_w_in_t(w):
    wt = jnp.swapaxes(w, 0, 1)
    parts = []
    for n in _PACK_ORDER:
        seg = wt[_IN_OFFS[n]:_IN_OFFS[n] + _IN_W[n]]
        pad = _PACK_W[n] - _IN_W[n]
        if pad:
            seg = jnp.pad(seg, ((0, pad), (0, 0)))
        parts.append(seg)
    return jnp.concatenate(parts, axis=0)


def _col(name, width):
    assert _PACK_OFF[name] % width == 0
    return _PACK_OFF[name] // width


def _sigmoid(x):
    return jax.nn.sigmoid(x)


def _silu(x):
    return x * jax.nn.sigmoid(x)


def _params(sem):
    return pltpu.CompilerParams(dimension_semantics=sem, vmem_limit_bytes=VMEM_LIMIT)


def _ada_kernel(c_ref, w_ref, b_ref, o_ref):
    c = c_ref[...]
    o_ref[...] = jnp.dot(_silu(c).astype(BF16), w_ref[...], preferred_element_type=F32) + b_ref[...]


def _ada(c, w_bf, b):
    n, d = c.shape
    nout = w_bf.shape[1]
    tn = D_MODEL
    return pl.pallas_call(
        _ada_kernel,
        grid=(nout // tn,),
        in_specs=[pl.BlockSpec((n, d), lambda j: (0, 0)),
                  pl.BlockSpec((d, tn), lambda j: (0, j)),
                  pl.BlockSpec((1, tn), lambda j: (0, j))],
        out_specs=pl.BlockSpec((n, tn), lambda j: (0, j)),
        out_shape=jax.ShapeDtypeStruct((n, nout), F32),
        compiler_params=_params(("arbitrary",)),
        name="ada",
    )(c, w_bf, b.reshape(1, nout))


def _prenorm_kernel(x_ref, g_ref, sc_ref, sh_ref, o_ref):
    x = x_ref[...]
    ms = jnp.mean(x * x, axis=-1, keepdims=True)
    y = x * lax.rsqrt(ms + EPS) * g_ref[...]
    o_ref[...] = (y * (1.0 + sc_ref[0]) + sh_ref[0]).astype(o_ref.dtype)


def _prenorm(x2, g, scale3, shift3, tm, tiles_per_mod):
    m, d = x2.shape
    r = scale3.shape[1]
    return pl.pallas_call(
        _prenorm_kernel,
        grid=(m // tm,),
        in_specs=[pl.BlockSpec((tm, d), lambda i: (i, 0)),
                  pl.BlockSpec((1, d), lambda i: (0, 0)),
                  pl.BlockSpec((1, r, d), lambda i: (i // tiles_per_mod, 0, 0)),
                  pl.BlockSpec((1, r, d), lambda i: (i // tiles_per_mod, 0, 0))],
        out_specs=pl.BlockSpec((tm, d), lambda i: (i, 0)),
        out_shape=jax.ShapeDtypeStruct((m, d), BF16),
        compiler_params=_params(("parallel",)),
        name="prenorm",
    )(x2, g.reshape(1, d), scale3, shift3)


def _mm_kernel(a_ref, bt_ref, o_ref):
    o_ref[...] = lax.dot_general(a_ref[...], bt_ref[...], (((1,), (1,)), ((), ())),
                                 preferred_element_type=F32)


def _matmul_nt(a_bf, bt_bf, tm, tn):
    m, k = a_bf.shape
    n = bt_bf.shape[0]
    return pl.pallas_call(
        _mm_kernel,
        grid=(n // tn, m // tm),
        in_specs=[pl.BlockSpec((tm, k), lambda j, i: (i, 0)),
                  pl.BlockSpec((tn, k), lambda j, i: (j, 0))],
        out_specs=pl.BlockSpec((tm, tn), lambda j, i: (i, j)),
        out_shape=jax.ShapeDtypeStruct((m, n), F32),
        compiler_params=_params(("parallel", "parallel")),
        name="inproj",
    )(a_bf, bt_bf)


def _ln_swish_gate(y, gate, bdw, lng, lnb):
    y = y + bdw
    mu = jnp.mean(y, axis=-1, keepdims=True)
    yc = y - mu
    var = jnp.mean(yc * yc, axis=-1, keepdims=True)
    yn = yc * lax.rsqrt(var + EPS) * lng + lnb
    return _silu(yn) * _silu(gate)


def _conv_prompt_kernel(val_ref, glu_ref, gate_ref, valh_ref, gluh_ref, buf_ref, w_ref, bdw_ref,
                        lng_ref, lnb_ref, ya_ref, utail_ref, f_ref, g_ref, *, tt):
    ti = pl.program_id(1)
    u = val_ref[0] * _sigmoid(glu_ref[0])
    uh = valh_ref[0] * _sigmoid(gluh_ref[0])
    f_ref[0:HALO, :] = jnp.where(ti == 0, buf_ref[0], uh)
    f_ref[HALO:HALO + tt, :] = u
    nrows = g_ref.shape[1]
    for p in range(1, SUBLANES):
        g_ref[p - 1] = f_ref[pl.ds(p, nrows), :]
    off = HALO - (CONV_W - 1)
    acc = jnp.zeros((tt, D_A), F32)
    for j in range(CONV_W):
        p = (off + j) % SUBLANES
        a = off + j - p
        win = f_ref[pl.ds(a, tt), :] if p == 0 else g_ref[p - 1, pl.ds(a, tt), :]
        acc = acc + w_ref[j:j + 1, :] * win
    ya_ref[0] = _ln_swish_gate(acc, gate_ref[0], bdw_ref[...], lng_ref[...], lnb_ref[...])

    @pl.when(ti == pl.num_programs(1) - 1)
    def _():
        utail_ref[0] = f_ref[tt:tt + HALO, :]


def _conv_prompt(z3, buf32, w_dw, b_dw, ln_g, ln_b, tt):
    b, t, _ = z3.shape
    hb = tt // HALO
    cur = lambda name: pl.BlockSpec((1, tt, D_A), lambda bi, ti, c=_col(name, D_A): (bi, ti, c))
    halo = lambda name: pl.BlockSpec(
        (1, HALO, D_A), lambda bi, ti, c=_col(name, D_A): (bi, jnp.maximum(ti * hb - 1, 0), c))
    vec = pl.BlockSpec((1, D_A), lambda bi, ti: (0, 0))
    wpad = jnp.pad(w_dw, ((0, HALO - CONV_W), (0, 0)))
    return pl.pallas_call(
        functools.partial(_conv_prompt_kernel, tt=tt),
        grid=(b, t // tt),
        in_specs=[cur("a_val"), cur("a_glu"), cur("a_gate"), halo("a_val"), halo("a_glu"),
                  pl.BlockSpec((1, HALO, D_A), lambda bi, ti: (bi, 0, 0)),
                  pl.BlockSpec((HALO, D_A), lambda bi, ti: (0, 0)), vec, vec, vec],
        out_specs=[pl.BlockSpec((1, tt, D_A), lambda bi, ti: (bi, ti, 0)),
                   pl.BlockSpec((1, HALO, D_A), lambda bi, ti: (bi, 0, 0))],
        out_shape=[jax.ShapeDtypeStruct((b, t, D_A), F32),
                   jax.ShapeDtypeStruct((b, HALO, D_A), F32)],
        scratch_shapes=[pltpu.VMEM((HALO + tt, D_A), F32),
                        pltpu.VMEM((SUBLANES - 1, HALO + tt - SUBLANES, D_A), F32)],
        compiler_params=_params(("parallel", "arbitrary")),
        name="conv_prompt",
    )(z3, z3, z3, z3, z3, buf32, wpad, b_dw.reshape(1, D_A), ln_g.reshape(1, D_A), ln_b.reshape(1, D_A))


def _conv_sample_kernel(val_ref, glu_ref, gate_ref, buf_ref, w_ref, bdw_ref, lng_ref, lnb_ref,
                        ya_ref, u_ref, *, ts):
    nb = CONV_W - 1
    for t in range(ts):
        u_ref[t] = val_ref[t] * _sigmoid(glu_ref[t])
    for t in range(ts):
        acc = jnp.zeros(u_ref.shape[1:], F32)
        for j in range(CONV_W):
            r = t + j
            src = buf_ref[r] if r < nb else u_ref[r - nb]
            acc = acc + w_ref[j:j + 1, :] * src
        ya_ref[t] = _ln_swish_gate(acc, gate_ref[t], bdw_ref[...], lng_ref[...], lnb_ref[...])


def _conv_sample(val_t, glu_t, gate_t, buf_t, w_dw, b_dw, ln_g, ln_b):
    ts, b, _ = val_t.shape
    assert ts <= CONV_W - 1
    return pl.pallas_call(
        functools.partial(_conv_sample_kernel, ts=ts),
        out_shape=[jax.ShapeDtypeStruct((ts, b, D_A), F32), jax.ShapeDtypeStruct((ts, b, D_A), F32)],
        compiler_params=pltpu.CompilerParams(vmem_limit_bytes=VMEM_LIMIT),
        name="conv_sample",
    )(val_t, glu_t, gate_t, buf_t, w_dw, b_dw.reshape(1, D_A), ln_g.reshape(1, D_A), ln_b.reshape(1, D_A))


def _lane_iota(shape):
    return lax.broadcasted_iota(I32, shape, len(shape) - 1)


def _swap_half(x):
    w = x.shape[-1]
    half = HEAD_DIM // 2
    first = (_lane_iota(x.shape) % HEAD_DIM) < half
    return jnp.where(first, pltpu.roll(x, w - half, 1), pltpu.roll(x, half, 1))


def _group_sum(s):
    same_group = (lax.broadcasted_iota(I32, (LANES, LANES), 0) // HEAD_DIM
                  == lax.broadcasted_iota(I32, (LANES, LANES), 1) // HEAD_DIM).astype(BF16)
    hi = s.astype(BF16)
    r1 = s - hi.astype(F32)
    mid = r1.astype(BF16)
    lo = (r1 - mid.astype(F32)).astype(BF16)
    cols = []
    for c in range(s.shape[-1] // LANES):
        sl = slice(c * LANES, (c + 1) * LANES)
        cols.append(jnp.dot(hi[:, sl], same_group, preferred_element_type=F32)
                    + jnp.dot(mid[:, sl], same_group, preferred_element_type=F32)
                    + jnp.dot(lo[:, sl], same_group, preferred_element_type=F32))
    return cols[0] if len(cols) == 1 else jnp.concatenate(cols, axis=-1)


def _tile_lanes(t, w):
    reps = w // t.shape[-1]
    return t if reps == 1 else jnp.concatenate([t] * reps, axis=-1)


def _rope(x, cos, sin):
    w = x.shape[-1]
    return x * _tile_lanes(cos, w) + _swap_half(x) * _tile_lanes(sin, w)


def _head_rms(x, g):
    ms = _group_sum(x * x) * (1.0 / HEAD_DIM)
    y = x * lax.rsqrt(ms + EPS)
    return y if g is None else y * _tile_lanes(g, x.shape[-1])


def _qkrope_kernel(zq_ref, zk_ref, zqi_ref, zki_ref, zv_ref, cos_ref, sin_ref, qg_ref, kg_ref,
                   q_ref, k_ref, qi_ref, ki_ref, v_ref, kb_ref, vb_ref, kib_ref, *, feature_major):
    cos = cos_ref[...]
    sin = sin_ref[...]
    q_ref[...] = _rope(_head_rms(zq_ref[...], qg_ref[...]), cos, sin)
    k = _rope(_head_rms(zk_ref[...], kg_ref[...]), cos, sin)
    kb_ref[...] = k.astype(BF16)
    qi_ref[...] = _rope(zqi_ref[...], cos, sin)
    ki = _rope(_head_rms(zki_ref[...], None), cos, sin)
    kib_ref[...] = ki[:, :IDX_DIM].astype(BF16)
    v = zv_ref[...]
    vb_ref[...] = v.astype(BF16)
    if feature_major:
        k_ref[0] = k.T
        ki_ref[0] = ki.T[:IDX_DIM]
        v_ref[0] = v.T
    else:
        k_ref[...] = k
        ki_ref[...] = ki[:, :IDX_DIM]
        v_ref[...] = v


def _qkrope(z, cos_t, sin_t, qg, kg, tm, table_tiles, feature_major):
    m = z.shape[0]
    zc = lambda name, w: pl.BlockSpec((tm, w), lambda i, c=_col(name, w): (i, c))
    tab = pl.BlockSpec((tm, LANES), lambda i: (i % table_tiles, 0))
    vec = pl.BlockSpec((1, LANES), lambda i: (0, 0))
    row = lambda w: pl.BlockSpec((tm, w), lambda i: (i, 0))
    qi_w = IDX_HEADS * IDX_DIM
    if feature_major:
        nseq, t = m // (tm * table_tiles), tm * table_tiles
        fm = lambda w: pl.BlockSpec((1, w, tm), lambda i: (i // table_tiles, 0, i % table_tiles))
        f32_spec = lambda w: fm(w)
        f32_shape = lambda w: jax.ShapeDtypeStruct((nseq, w, t), F32)
    else:
        f32_spec = row
        f32_shape = lambda w: jax.ShapeDtypeStruct((m, w), F32)
    return pl.pallas_call(
        functools.partial(_qkrope_kernel, feature_major=feature_major),
        grid=(m // tm,),
        in_specs=[zc("zq", D_B), zc("zk", KV_DIM), zc("zqi", qi_w), zc("zki", LANES), zc("zv", KV_DIM),
                  tab, tab, vec, vec],
        out_specs=[row(D_B), f32_spec(KV_DIM), row(qi_w), f32_spec(IDX_DIM), f32_spec(KV_DIM),
                   row(KV_DIM), row(KV_DIM), row(IDX_DIM)],
        out_shape=[jax.ShapeDtypeStruct((m, D_B), F32), f32_shape(KV_DIM),
                   jax.ShapeDtypeStruct((m, qi_w), F32), f32_shape(IDX_DIM), f32_shape(KV_DIM),
                   jax.ShapeDtypeStruct((m, KV_DIM), BF16), jax.ShapeDtypeStruct((m, KV_DIM), BF16),
                   jax.ShapeDtypeStruct((m, IDX_DIM), BF16)],
        compiler_params=_params(("parallel",)),
        name="qkrope",
    )(z, z, z, z, z, cos_t, sin_t, qg, kg)


def _rope_tables(pos):
    half = HEAD_DIM // 2
    inv = ROPE_THETA ** (-jnp.arange(half, dtype=F32) / half)
    ang = pos.astype(F32)[:, None] * inv[None, :]
    cos = jnp.cos(ang)
    sin = jnp.sin(ang)
    cos64 = jnp.concatenate([cos, cos], axis=1)
    sin64 = jnp.concatenate([-sin, sin], axis=1)
    reps = LANES // HEAD_DIM
    return jnp.tile(cos64, (1, reps)), jnp.tile(sin64, (1, reps))


_SIGN = np.int32(-2 ** 31)
_MAG = np.int32(0x7FFFFFFF)
QK_SCALE = HEAD_DIM ** -0.5 * math.log2(math.e)
SEARCH_UNROLL = 4
SAMPLE_DIGIT_BITS = 3


def _dot_nt(a, b):
    return lax.dot_general(a, b, (((1,), (1,)), ((), ())), preferred_element_type=F32)


def _row_count(mask):
    return jnp.sum(mask.astype(F32), axis=1, keepdims=True)


def _ordered_to_float(t):
    key = t ^ _SIGN
    return lax.bitcast_convert_type(jnp.where(key < 0, key ^ _MAG, key), F32)


def _threshold_bits(sc_ref, kf):
    nq, nl = sc_ref.shape

    def cond(c):
        i, _, cnt_t = c
        return (i < 32) & (jnp.max(jnp.abs(cnt_t - kf)) > 0.0)

    def body(c):
        i, t, cnt_t = c
        for j in range(SEARCH_UNROLL):
            cand = t | jnp.left_shift(jnp.int32(1), 31 - (i + j))
            cnt = _row_count(sc_ref[...] >= _ordered_to_float(cand))
            ok = cnt >= kf
            t = jnp.where(ok, cand, t)
            cnt_t = jnp.where(ok, cnt, cnt_t)
        return i + SEARCH_UNROLL, t, cnt_t

    init = (jnp.int32(0), jnp.zeros((nq, 1), I32), jnp.full((nq, 1), float(nl), F32))
    return lax.while_loop(cond, body, init)[1:]


def _threshold_digits(sc_ref, kf, digit_bits):
    nq, nl = sc_ref.shape
    t = jnp.zeros((nq, 1), I32)
    cnt_t = jnp.full((nq, 1), float(nl), F32)
    pos = 32
    while pos > 0:
        nb = (pos % digit_bits) or digit_bits
        pos -= nb
        digit = jnp.zeros((nq, 1), I32)
        for j in range(1, 2 ** nb):
            cand = t | np.uint32(j << pos).astype(np.int32)
            cnt = _row_count(sc_ref[...] >= _ordered_to_float(cand))
            ok = cnt >= kf
            digit = digit + ok.astype(I32)
            cnt_t = jnp.where(ok, cnt, cnt_t)
        t = t | jnp.left_shift(digit, pos)
    return t, cnt_t


def _select_topk(sc_ref, sel_ref, topk, digit_bits=1):
    nq, nl = sc_ref.shape
    if nl == topk:
        sel_ref[...] = jnp.ones((nq, nl), F32)
        return
    kf = float(topk)
    t, cnt_t = (_threshold_bits(sc_ref, kf) if digit_bits == 1
                else _threshold_digits(sc_ref, kf, digit_bits))
    tau = jnp.where(t == 0, -jnp.inf, _ordered_to_float(t))
    sel_ref[...] = (sc_ref[...] >= tau).astype(F32)
    tie_rows = (cnt_t > kf) & (tau > NEG)

    @pl.when(jnp.max(tie_rows.astype(F32)) > 0.0)
    def _():
        need = kf - _row_count(sc_ref[...] > tau)
        tri = (lax.broadcasted_iota(I32, (LANES, LANES), 0)
               < lax.broadcasted_iota(I32, (LANES, LANES), 1)).astype(BF16)
        run = jnp.zeros((nq, 1), F32)
        for j in range(nl // LANES):
            sl = slice(j * LANES, (j + 1) * LANES)
            sj = sc_ref[:, sl]
            eqj = sj == tau
            rank = jnp.dot(eqj.astype(BF16), tri, preferred_element_type=F32) + run
            sel_ref[:, sl] = ((sj > tau) | (eqj & (rank < need))).astype(F32)
            run = run + _row_count(eqj)


def _dsa_core(q, qi, wi, bgate, qpos, kb_ref, vb_ref, kib_ref, sc_ref, sel_ref, topk):
    nq = q.shape[0]
    nl = kb_ref.shape[0]
    kib = kib_ref[...]
    qis = (qi * (IDX_DIM ** -0.5)).astype(BF16)
    wis = wi * (IDX_HEADS ** -0.5)
    scores = jnp.zeros((nq, nl), F32)
    for h in range(IDX_HEADS):
        s = _dot_nt(qis[:, h * IDX_DIM:(h + 1) * IDX_DIM], kib)
        scores = scores + jnp.maximum(s, 0.0) * wis[:, h:h + 1]
    causal = _lane_iota((nq, nl)) <= qpos
    sc_ref[...] = jnp.where(causal, scores, NEG)
    _select_topk(sc_ref, sel_ref, topk)

    sel = (sel_ref[...] > 0.0) & causal
    lane = _lane_iota((nq, LANES))
    group_w = N_HEADS // N_KV_HEADS
    kb = kb_ref[...]
    vb = vb_ref[...]
    qs = q * QK_SCALE
    outs = []
    for h in range(N_HEADS):
        g = h // group_w
        c = (h * HEAD_DIM) // LANES
        x = qs[:, c * LANES:(c + 1) * LANES]
        if (h % 2) != g:
            x = pltpu.roll(x, HEAD_DIM, 1)
        in_g = (lane // HEAD_DIM) == g
        xq = jnp.where(in_g, x, 0.0).astype(BF16)
        s = jnp.where(sel, _dot_nt(xq, kb), NEG)
        mx = jnp.max(s, axis=1, keepdims=True)
        p = jnp.exp2(s - mx)
        den = jnp.sum(p, axis=1, keepdims=True)
        o = jnp.dot(p.astype(BF16), vb, preferred_element_type=F32) / den
        o = jnp.where(in_g, o, 0.0)
        if (h % 2) != g:
            o = pltpu.roll(o, HEAD_DIM, 1)
        outs.append(o)
    cols = [outs[2 * c] + outs[2 * c + 1] for c in range(N_HEADS // 2)]
    return jnp.concatenate(cols, axis=1) * _silu(bgate)


def _dsa_prompt_kernel(q_ref, qi_ref, wi_ref, bg_ref, kb_ref, vb_ref, kib_ref, o_ref, sc_ref, sel_ref,
                       *, topk, q_lo):
    qb = pl.program_id(1) + q_lo
    nq = q_ref.shape[1]
    qpos = qb * nq + lax.broadcasted_iota(I32, (nq, 1), 0)
    o_ref[0] = _dsa_core(q_ref[0], qi_ref[0], wi_ref[0], bg_ref[0], qpos,
                         kb_ref.at[0], vb_ref.at[0], kib_ref.at[0], sc_ref, sel_ref, topk)


def _dsa_prompt_bucket(q3, qi3, z3, kb3, vb3, kib3, nq, q_lo, q_hi):
    b, t, _ = q3.shape
    topk = min(TOPK_MAX, t // 4)
    nl = q_hi * nq
    qi_w = IDX_HEADS * IDX_DIM
    qblk = lambda w: pl.BlockSpec((1, nq, w), lambda bi, i: (bi, i + q_lo, 0))
    zblk = lambda name, w: pl.BlockSpec((1, nq, w), lambda bi, i, c=_col(name, w): (bi, i + q_lo, c))
    keys = lambda w: pl.BlockSpec((1, nl, w), lambda bi, i: (bi, 0, 0))
    return pl.pallas_call(
        functools.partial(_dsa_prompt_kernel, topk=topk, q_lo=q_lo),
        grid=(b, q_hi - q_lo),
        in_specs=[qblk(D_B), qblk(qi_w), zblk("zwi", LANES), zblk("b_gate", D_B),
                  keys(KV_DIM), keys(KV_DIM), keys(IDX_DIM)],
        out_specs=pl.BlockSpec((1, nq, D_B), lambda bi, i: (bi, i, 0)),
        out_shape=jax.ShapeDtypeStruct((b, (q_hi - q_lo) * nq, D_B), F32),
        scratch_shapes=[pltpu.VMEM((nq, nl), F32), pltpu.VMEM((nq, nl), F32)],
        compiler_params=_params(("parallel", "parallel")),
        name="dsa_prompt",
    )(q3, qi3, z3, z3, kb3, vb3, kib3)


DSA_BUCKETS = 8
DSA_ROWS = 256


def _dsa_prompt(q3, qi3, z3, kb3, vb3, kib3):
    t = q3.shape[1]
    nq = DSA_ROWS if t % DSA_ROWS == 0 else Q_BLOCK
    nblk = t // nq
    step = max(1, nblk // DSA_BUCKETS)
    parts = [_dsa_prompt_bucket(q3, qi3, z3, kb3, vb3, kib3, nq, lo, min(lo + step, nblk))
             for lo in range(0, nblk, step)]
    return parts[0] if len(parts) == 1 else jnp.concatenate(parts, axis=1)


def _dsa_sample_kernel(pt_ref, q_ref, qi_ref, wi_ref, bg_ref, knt_ref, vnt_ref, kint_ref,
                       ck_hbm, cv_hbm, cki_hbm, o_ref, kt_ref, vt_ref, kit_ref, sc_ref, sel_ref, sems,
                       *, topk, n_pages, page_base, ts):
    bi = pl.program_id(0)
    nb = pl.num_programs(0)
    nq = q_ref.shape[1]
    past = n_pages * PAGE_SIZE
    nl = kt_ref.shape[2]
    group_w = N_HEADS // N_KV_HEADS
    slot = bi % 2

    def copies(seq, s, p):
        page = pt_ref[seq, p] + page_base
        dst = pl.ds(p * PAGE_SIZE, PAGE_SIZE)
        return (pltpu.make_async_copy(ck_hbm.at[page], kt_ref.at[s, :, dst], sems.at[s, 0]),
                pltpu.make_async_copy(cv_hbm.at[page], vt_ref.at[s, :, dst], sems.at[s, 1]),
                pltpu.make_async_copy(cki_hbm.at[page], kit_ref.at[s, :, dst], sems.at[s, 2]))

    def start_gather(seq, s):
        for p in range(n_pages):
            for cp in copies(seq, s, p):
                cp.start()

    @pl.when(bi == 0)
    def _():
        start_gather(0, 0)

    @pl.when(bi + 1 < nb)
    def _():
        start_gather(bi + 1, 1 - slot)

    kt_ref[slot, :, past:nl] = knt_ref[0]
    vt_ref[slot, :, past:nl] = vnt_ref[0]
    kit_ref[slot, :, past:nl] = kint_ref[0]

    for p in range(n_pages):
        for cp in copies(bi, slot, p):
            cp.wait()
    kt_ref, vt_ref, kit_ref = kt_ref.at[slot], vt_ref.at[slot], kit_ref.at[slot]

    row = lax.broadcasted_iota(I32, (nq, 1), 0)
    qpos = past + jnp.minimum(row, ts - 1)
    stack = lambda x, w, heads: jnp.concatenate([x[:, h * w:(h + 1) * w] for h in heads], axis=0)

    qis = stack(qi_ref[0] * (IDX_DIM ** -0.5), IDX_DIM, range(IDX_HEADS)).astype(BF16)
    s = jnp.dot(qis, kit_ref[...].astype(BF16), preferred_element_type=F32)
    wis = wi_ref[0] * (IDX_HEADS ** -0.5)
    scores = jnp.zeros((nq, nl), F32)
    for h in range(IDX_HEADS):
        scores = scores + jnp.maximum(s[h * nq:(h + 1) * nq], 0.0) * wis[:, h:h + 1]
    causal = _lane_iota((nq, nl)) <= qpos
    sc_ref[...] = jnp.where(causal, scores, NEG)
    _select_topk(sc_ref, sel_ref, topk, digit_bits=SAMPLE_DIGIT_BITS)

    sel = (sel_ref[...] > 0.0) & causal
    sel_g = jnp.concatenate([sel] * group_w, axis=0)
    qs = q_ref[0] * QK_SCALE
    outs = []
    for g in range(N_KV_HEADS):
        rows = slice(g * HEAD_DIM, (g + 1) * HEAD_DIM)
        qg = stack(qs, HEAD_DIM, range(g * group_w, (g + 1) * group_w)).astype(BF16)
        sg = jnp.dot(qg, kt_ref[rows, :].astype(BF16), preferred_element_type=F32)
        sg = jnp.where(sel_g, sg, NEG)
        mx = jnp.max(sg, axis=1, keepdims=True)
        p = jnp.exp2(sg - mx)
        den = jnp.sum(p, axis=1, keepdims=True)
        og = _dot_nt(p.astype(BF16), vt_ref[rows, :].astype(BF16)) / den
        outs += [og[j * nq:(j + 1) * nq] for j in range(group_w)]
    o_ref[0] = jnp.concatenate(outs, axis=1) * _silu(bg_ref[0])


def _dsa_sample(page_table, q8, qi8, wi8, bg8, knt, vnt, kint, ckt, cvt, ckit, layer, depth, ts):
    b, nq, _ = q8.shape
    n_pages = page_table.shape[1]
    past = n_pages * PAGE_SIZE
    nl = past + LANES
    topk = min(TOPK_MAX, (past + ts) // 4)
    n_pool = ckt.shape[0] // depth
    qi_w = IDX_HEADS * IDX_DIM
    blk = lambda r, w: pl.BlockSpec((1, r, w), lambda bi, pt: (bi, 0, 0))
    anyspec = pl.BlockSpec(memory_space=pl.ANY)
    grid_spec = pltpu.PrefetchScalarGridSpec(
        num_scalar_prefetch=1,
        grid=(b,),
        in_specs=[blk(nq, D_B), blk(nq, qi_w), blk(nq, LANES), blk(nq, D_B),
                  blk(KV_DIM, LANES), blk(KV_DIM, LANES), blk(IDX_DIM, LANES),
                  anyspec, anyspec, anyspec],
        out_specs=pl.BlockSpec((1, nq, D_B), lambda bi, pt: (bi, 0, 0)),
        scratch_shapes=[pltpu.VMEM((2, KV_DIM, nl), F32), pltpu.VMEM((2, KV_DIM, nl), F32),
                        pltpu.VMEM((2, IDX_DIM, nl), F32),
                        pltpu.VMEM((nq, nl), F32), pltpu.VMEM((nq, nl), F32),
                        pltpu.SemaphoreType.DMA((2, 3))])
    return pl.pallas_call(
        functools.partial(_dsa_sample_kernel, topk=topk, n_pages=n_pages, page_base=layer * n_pool, ts=ts),
        grid_spec=grid_spec,
        out_shape=jax.ShapeDtypeStruct((b, nq, D_B), F32),
        compiler_params=_params(("arbitrary",)),
        name="dsa_sample",
    )(page_table, q8, qi8, wi8, bg8, knt, vnt, kint, ckt, cvt, ckit)


def _hgrn_kernel(cq_ref, cf_ref, ci_ref, cg_ref, lb_ref, ng_ref, s0_ref, yc_ref, s_ref, *, c, valid):
    ci_idx = pl.program_id(1)

    @pl.when(ci_idx == 0)
    def _():
        s_ref[...] = s0_ref[...]

    for r in range(cq_ref.shape[0]):
        _hgrn_chunk(r, cq_ref, cf_ref, ci_ref, cg_ref, lb_ref, ng_ref, yc_ref, s_ref, c, valid)


def _hgrn_chunk(r, cq_ref, cf_ref, ci_ref, cg_ref, lb_ref, ng_ref, yc_ref, s_ref, c, valid):
    fx = cf_ref[r]
    lb = lb_ref[...]
    log_f = jnp.log(jnp.maximum(lb, LB_FLOOR) + (1.0 - lb) * _sigmoid(fx))
    kk = (1.0 - lb) * _sigmoid(-fx)
    qc = _silu(cq_ref[r])
    iv = ci_ref[r]
    row = lax.broadcasted_iota(I32, (c, 1), 0)
    if valid < c:
        log_f = jnp.where(row < valid, log_f, 0.0)

    rr = lax.broadcasted_iota(I32, (c, c), 0)
    cc = lax.broadcasted_iota(I32, (c, c), 1)
    hs = [slice(h * C_KDIM, (h + 1) * C_KDIM) for h in range(C_HEADS)]
    att = [jnp.zeros((c, c), F32) for _ in range(C_HEADS)]
    cs = log_f
    tot = log_f
    m = 1
    while m < c:
        right = ((row // m) % 2) == 1
        qm = jnp.where(right, qc * jnp.exp(cs), 0.0).astype(BF16)
        km = jnp.where(right, 0.0, kk * jnp.exp(tot - cs)).astype(BF16)
        pair = (rr // (2 * m)) == (cc // (2 * m))
        for h in range(C_HEADS):
            att[h] = att[h] + jnp.where(pair, _dot_nt(qm[:, hs[h]], km[:, hs[h]]), 0.0)
        sib = jnp.where(right, pltpu.roll(tot, m, 0), pltpu.roll(tot, c - m, 0))
        cs = jnp.where(right, cs + sib, cs)
        tot = tot + sib
        m *= 2
    qdec = (qc * jnp.exp(cs)).astype(BF16)
    kdec = (kk * jnp.exp(tot - cs)).astype(BF16)
    ivb = iv.astype(BF16)
    eye_c = rr == cc
    eye_k = (lax.broadcasted_iota(I32, (C_KDIM, C_KDIM), 0)
             == lax.broadcasted_iota(I32, (C_KDIM, C_KDIM), 1))
    ys = []
    for h in range(C_HEADS):
        sl = hs[h]
        diag = jnp.sum(qc[:, sl] * kk[:, sl], axis=1, keepdims=True)
        a_h = att[h] + jnp.where(eye_c, diag, 0.0)
        s_h = s_ref[r, h]
        o = (jnp.dot(a_h.astype(BF16), ivb[:, sl], preferred_element_type=F32)
             + jnp.dot(qdec[:, sl], s_h.astype(BF16), preferred_element_type=F32))
        e_end = jnp.exp(tot[0:1, sl])
        e_col = jnp.sum(jnp.where(eye_k, e_end, 0.0), axis=1, keepdims=True)
        upd = lax.dot_general(kdec[:, sl], ivb[:, sl], (((0,), (0,)), ((), ())),
                              preferred_element_type=F32)
        s_ref[r, h] = e_col * s_h + upd
        ms = jnp.mean(o * o, axis=-1, keepdims=True)
        ys.append(o * lax.rsqrt(ms + EPS) * ng_ref[...])
    yc_ref[r] = jnp.concatenate(ys, axis=1) * _silu(cg_ref[r])


HGRN_SEQS = 4


def _hgrn(z3, lb, cng, s0, c, valid):
    b, t, _ = z3.shape
    nb = math.gcd(b, HGRN_SEQS)
    zblk = lambda name: pl.BlockSpec((nb, c, D_C), lambda bi, i, col=_col(name, D_C): (bi, i, col))
    sblk = pl.BlockSpec((nb, C_HEADS, C_KDIM, C_VDIM), lambda bi, i: (bi, 0, 0, 0))
    return pl.pallas_call(
        functools.partial(_hgrn_kernel, c=c, valid=valid),
        grid=(b // nb, t // c),
        in_specs=[zblk("cq"), zblk("cf"), zblk("ci"), zblk("c_gate"),
                  pl.BlockSpec((1, C_FDIM), lambda bi, i: (0, 0)),
                  pl.BlockSpec((1, C_VDIM), lambda bi, i: (0, 0)), sblk],
        out_specs=[pl.BlockSpec((nb, c, D_C), lambda bi, i: (bi, i, 0)), sblk],
        out_shape=[jax.ShapeDtypeStruct((b, t, D_C), F32),
                   jax.ShapeDtypeStruct((b, C_HEADS, C_KDIM, C_VDIM), F32)],
        compiler_params=_params(("parallel", "arbitrary")),
        name="hgrn",
    )(z3, z3, z3, z3, lb.reshape(1, C_FDIM), cng.reshape(1, C_VDIM), s0)


def _merge_kernel(x_ref, ya_ref, yb_ref, yc_ref, ga_ref, gb_ref, gc_ref, gate_ref,
                  wa_ref, wb_ref, wc_ref, wo_ref, o_ref):
    def proj(y_ref, w_ref):
        return jnp.dot(y_ref[...].astype(BF16), w_ref[...], preferred_element_type=F32)

    m = (_sigmoid(ga_ref[...]) * proj(ya_ref, wa_ref)
         + _sigmoid(gb_ref[...]) * proj(yb_ref, wb_ref)
         + _sigmoid(gc_ref[...]) * proj(yc_ref, wc_ref))
    o_ref[...] = x_ref[...] + gate_ref[0] * jnp.dot(m.astype(BF16), wo_ref[...], preferred_element_type=F32)


def _merge(x2, ya, yb, yc, z, gate3, wa, wb, wc, wo, tm, tiles_per_mod):
    m, d = x2.shape
    r = gate3.shape[1]
    row = lambda w: pl.BlockSpec((tm, w), lambda i: (i, 0))
    zc = lambda name: pl.BlockSpec((tm, d), lambda i, c=_col(name, d): (i, c))
    wspec = lambda k: pl.BlockSpec((k, d), lambda i: (0, 0))
    return pl.pallas_call(
        _merge_kernel,
        grid=(m // tm,),
        in_specs=[row(d), row(D_A), row(D_B), row(D_C), zc("g_a"), zc("g_b"), zc("g_c"),
                  pl.BlockSpec((1, r, d), lambda i: (i // tiles_per_mod, 0, 0)),
                  wspec(D_A), wspec(D_B), wspec(D_C), wspec(d)],
        out_specs=row(d),
        out_shape=jax.ShapeDtypeStruct((m, d), F32),
        compiler_params=_params(("parallel",)),
        name="merge",
    )(x2, ya, yb, yc, z, z, z, gate3, wa, wb, wc, wo)


def _pick_tile(n, pref):
    t = min(pref, n)
    while n % t:
        t //= 2
    return t


def kernel(x_prompt, x_sample, cache_k, cache_v, cache_idx_k, state_conv, state_hgrn, page_table,
           c_prompt, c_sample, w_ada, b_ada, norm_g, w_in, w_dw, b_dw, ln_g, ln_b, q_norm_g, k_norm_g,
           lb_logits, c_norm_g, w_proj_a, w_proj_b, w_proj_c, w_out):
    depth = w_in.shape[0]
    bp, tp, d = x_prompt.shape
    bs, ts, _ = x_sample.shape
    mp, ms = bp * tp, bs * ts
    n_pages = page_table.shape[1]
    past = n_pages * PAGE_SIZE
    assert d == D_MODEL and tp % Q_BLOCK == 0 and tp % CHUNK == 0 and tp >= HALO
    assert ts <= SUBLANES and ms % SUBLANES == 0

    lbp = jax.nn.softmax(lb_logits.astype(F32), axis=0)
    lb_all = jnp.cumsum(lbp, axis=0) - lbp[0:1]
    cos_p, sin_p = _rope_tables(jnp.arange(tp))
    cos_s, sin_s = _rope_tables(past + (jnp.arange(ms) % ts))
    ckt = jnp.transpose(cache_k, (0, 1, 3, 4, 2)).reshape(-1, KV_DIM, PAGE_SIZE)
    cvt = jnp.transpose(cache_v, (0, 1, 3, 4, 2)).reshape(-1, KV_DIM, PAGE_SIZE)
    ckit = jnp.transpose(cache_idx_k, (0, 1, 3, 2)).reshape(-1, IDX_DIM, PAGE_SIZE)
    c_all = jnp.concatenate([c_prompt, c_sample], axis=0)

    tm_p = _pick_tile(mp, 1024)
    tm_p = _pick_tile(tp, tm_p)
    tm_in = _pick_tile(tp, 2048)
    tt = _pick_tile(tp, 512)
    tm_g = _pick_tile(tp, 512)
    zeros_buf = jnp.zeros((bp, HALO, D_A), F32)
    s00 = jnp.zeros((bp, C_HEADS, C_KDIM, C_VDIM), F32)
    pad_rows = lambda a3: jnp.pad(a3, ((0, 0), (0, SUBLANES - ts), (0, 0)))

    xp = x_prompt.reshape(mp, d)
    xs = x_sample.reshape(ms, d)
    outs = [[] for _ in range(10)]
    for l in range(depth):
        mod = _ada(c_all, w_ada[l].astype(BF16), b_ada[l])
        shift, scale, gate = mod[:, :d], mod[:, d:2 * d], mod[:, 2 * d:]
        w_in_t = _pack_w_in_t(w_in[l]).astype(BF16)
        wa, wb, wc, wo = (w_proj_a[l].astype(BF16), w_proj_b[l].astype(BF16),
                          w_proj_c[l].astype(BF16), w_out[l].astype(BF16))
        qg = jnp.tile(q_norm_g[l], LANES // HEAD_DIM).reshape(1, LANES)
        kg = jnp.tile(k_norm_g[l], LANES // HEAD_DIM).reshape(1, LANES)
        lb = lb_all[l]

        per_seq = lambda a: a[:bp].reshape(bp, 1, d)
        h = _prenorm(xp, norm_g[l], per_seq(scale), per_seq(shift), tm_p, tp // tm_p)
        z = _matmul_nt(h, w_in_t, tm_in, NZ // 6)
        z3 = z.reshape(bp, tp, NZ)
        ya, utail = _conv_prompt(z3, zeros_buf, w_dw[l], b_dw[l], ln_g[l], ln_b[l], tt)
        q, kt, qi, kit, vt, kb, vb, kib = _qkrope(z, cos_p, sin_p, qg, kg, tm_p, tp // tm_p, True)
        r3 = lambda a: a.reshape(bp, tp, a.shape[-1])
        yb = _dsa_prompt(r3(q), r3(qi), z3, r3(kb), r3(vb), r3(kib))
        yc, s_p = _hgrn(z3, lb, c_norm_g[l], s00, CHUNK, CHUNK)
        xp = _merge(xp, ya.reshape(mp, D_A), yb.reshape(mp, D_B), yc.reshape(mp, D_C), z,
                    per_seq(gate), wa, wb, wc, wo, tm_g, tp // tm_g)
        heads_last = lambda a: jnp.transpose(a.reshape(bp, N_KV_HEADS, HEAD_DIM, tp), (0, 3, 1, 2))
        outs[0].append(heads_last(kt))
        outs[1].append(heads_last(vt))
        outs[2].append(jnp.swapaxes(kit, 1, 2))
        outs[3].append(utail[:, HALO - (CONV_W - 1):, :])
        outs[4].append(s_p)

        per_row = lambda a: jnp.repeat(a[bp:], ts, axis=0).reshape(1, ms, d)
        hs = _prenorm(xs, norm_g[l], per_row(scale), per_row(shift), ms, 1)
        zs = _matmul_nt(hs, w_in_t, ms, NZ // 6)
        zs3 = zs.reshape(bs, ts, NZ)
        seg_t = lambda name: jnp.swapaxes(
            zs3[:, :, _PACK_OFF[name]:_PACK_OFF[name] + D_A], 0, 1)
        ya_t, u_t = _conv_sample(seg_t("a_val"), seg_t("a_glu"), seg_t("a_gate"),
                                 jnp.swapaxes(state_conv[l], 0, 1), w_dw[l], b_dw[l], ln_g[l], ln_b[l])
        ya_s = jnp.swapaxes(ya_t, 0, 1).reshape(ms, D_A)
        q, k, qi, ki, v, _, _, _ = _qkrope(zs, cos_s, sin_s, qg, kg, ms, 1, False)
        p8 = lambda a: pad_rows(a.reshape(bs, ts, a.shape[-1]))
        zs8 = pad_rows(zs3)
        seg8 = lambda name, w: zs8[:, :, _PACK_OFF[name]:_PACK_OFF[name] + w]
        new_t = lambda a: jnp.pad(jnp.swapaxes(a.reshape(bs, ts, a.shape[-1]), 1, 2),
                                  ((0, 0), (0, 0), (0, LANES - ts)))
        yb8 = _dsa_sample(page_table, p8(q), p8(qi), seg8("zwi", LANES), seg8("b_gate", D_B),
                          new_t(k), new_t(v), new_t(ki), ckt, cvt, ckit, l, depth, ts)
        yc8, s_s = _hgrn(zs8, lb, c_norm_g[l], state_hgrn[l], SUBLANES, ts)
        xs = _merge(xs, ya_s, yb8[:, :ts].reshape(ms, D_B), yc8[:, :ts].reshape(ms, D_C), zs,
                    per_row(gate), wa, wb, wc, wo, ms, 1)
        outs[5].append(k.reshape(bs, ts, N_KV_HEADS, HEAD_DIM))
        outs[6].append(v.reshape(bs, ts, N_KV_HEADS, HEAD_DIM))
        outs[7].append(ki.reshape(bs, ts, IDX_DIM))
        outs[8].append(jnp.concatenate([state_conv[l][:, ts:], jnp.swapaxes(u_t, 0, 1)], axis=1))
        outs[9].append(s_s)

    st = [jnp.stack(o) for o in outs]
    return (xp.reshape(bp, tp, d), xs.reshape(bs, ts, d), st[0], st[1], st[2], st[3], st[4],
            st[5], st[6], st[7], st[8], st[9])
```

```python
import functools
import math

import jax
import jax.numpy as jnp
import numpy as np
from jax import lax
from jax.experimental import pallas as pl
from jax.experimental.pallas import tpu as pltpu

F32 = jnp.float32
BF16 = jnp.bfloat16
I32 = jnp.int32

LANES = 128
SUBLANES = 8
VMEM_LIMIT = 56 * 1024 * 1024

D_MODEL = 1024
EPS = 1e-6
NEG = -1e30
LB_FLOOR = 1e-30
D_A = D_MODEL // 2
CONV_W = 31
HALO = 32
N_HEADS = 8
N_KV_HEADS = 2
HEAD_DIM = 64
D_B = N_HEADS * HEAD_DIM
KV_DIM = N_KV_HEADS * HEAD_DIM
IDX_HEADS = 4
IDX_DIM = 64
TOPK_MAX = 256
ROPE_THETA = 10000.0
Q_BLOCK = 128
C_HEADS = 4
C_KDIM = 128
C_VDIM = D_MODEL // 2 // C_HEADS
D_C = C_HEADS * C_VDIM
C_FDIM = C_HEADS * C_KDIM
CHUNK = 64
PAGE_SIZE = 128

_IN_NAMES = ("a_val", "a_glu", "a_gate", "zq", "zk", "zv", "zqi", "zki", "zwi", "b_gate",
             "cq", "cf", "ci", "c_gate", "g_a", "g_b", "g_c")
_IN_WIDTHS = (D_A, D_A, D_A, D_B, KV_DIM, KV_DIM, IDX_HEADS * IDX_DIM, IDX_DIM, IDX_HEADS, D_B,
              C_FDIM, C_FDIM, D_C, D_C, D_MODEL, D_MODEL, D_MODEL)
_IN_OFFS = dict(zip(_IN_NAMES, np.concatenate([[0], np.cumsum(_IN_WIDTHS)[:-1]]).tolist()))
_IN_W = dict(zip(_IN_NAMES, _IN_WIDTHS))
_PACK_ORDER = ("g_a", "g_b", "g_c", "a_val", "a_glu", "a_gate", "zq", "b_gate", "cq", "cf", "ci",
               "c_gate", "zqi", "zk", "zv", "zki", "zwi")


def _round_up(n, m):
    return (n + m - 1) // m * m


_PACK_W = {n: _round_up(_IN_W[n], LANES) for n in _PACK_ORDER}
_PACK_OFF = {}
_o = 0
for _n in _PACK_ORDER:
    assert _o % _PACK_W[_n] == 0
    _PACK_OFF[_n] = _o
    _o += _PACK_W[_n]
NZ = _o


def _pack_w_in_t(w):
    wt = jnp.swapaxes(w, 0, 1)
    parts = []
    for n in _PACK_ORDER:
        seg = wt[_IN_OFFS[n]:_IN_OFFS[n] + _IN_W[n]]
        pad = _PACK_W[n] - _IN_W[n]
        if pad:
            seg = jnp.pad(seg, ((0, pad), (0, 0)))
        parts.append(seg)
    return jnp.concatenate(parts, axis=0)


def _col(name, width):
    assert _PACK_OFF[name] % width == 0
    return _PACK_OFF[name] // width


def _sigmoid(x):
    return jax.nn.sigmoid(x)


def _silu(x):
    return x * jax.nn.sigmoid(x)


def _params(sem):
    return pltpu.CompilerParams(dimension_semantics=sem, vmem_limit_bytes=VMEM_LIMIT)


def _ada_kernel(c_ref, w_ref, b_ref, o_ref):
    c = c_ref[...]
    o_ref[...] = jnp.dot(_silu(c).astype(BF16), w_ref[...], preferred_element_type=F32) + b_ref[...]


def _ada(c, w_bf, b):
    n, d = c.shape
    nout = w_bf.shape[1]
    tn = D_MODEL
    return pl.pallas_call(
        _ada_kernel,
        grid=(nout // tn,),
        in_specs=[pl.BlockSpec((n, d), lambda j: (0, 0)),
                  pl.BlockSpec((d, tn), lambda j: (0, j)),
                  pl.BlockSpec((1, tn), lambda j: (0, j))],
        out_specs=pl.BlockSpec((n, tn), lambda j: (0, j)),
        out_shape=jax.ShapeDtypeStruct((n, nout), F32),
        compiler_params=_params(("arbitrary",)),
        name="ada",
    )(c, w_bf, b.reshape(1, nout))


def _prenorm_kernel(x_ref, g_ref, sc_ref, sh_ref, o_ref):
    x = x_ref[...]
    ms = jnp.mean(x * x, axis=-1, keepdims=True)
    y = x * lax.rsqrt(ms + EPS) * g_ref[...]
    o_ref[...] = (y * (1.0 + sc_ref[0]) + sh_ref[0]).astype(o_ref.dtype)


def _prenorm(x2, g, scale3, shift3, tm, tiles_per_mod):
    m, d = x2.shape
    r = scale3.shape[1]
    return pl.pallas_call(
        _prenorm_kernel,
        grid=(m // tm,),
        in_specs=[pl.BlockSpec((tm, d), lambda i: (i, 0)),
                  pl.BlockSpec((1, d), lambda i: (0, 0)),
                  pl.BlockSpec((1, r, d), lambda i: (i // tiles_per_mod, 0, 0)),
                  pl.BlockSpec((1, r, d), lambda i: (i // tiles_per_mod, 0, 0))],
        out_specs=pl.BlockSpec((tm, d), lambda i: (i, 0)),
        out_shape=jax.ShapeDtypeStruct((m, d), BF16),
        compiler_params=_params(("parallel",)),
        name="prenorm",
    )(x2, g.reshape(1, d), scale3, shift3)


def _mm_kernel(a_ref, bt_ref, o_ref):
    o_ref[...] = lax.dot_general(a_ref[...], bt_ref[...], (((1,), (1,)), ((), ())),
                                 preferred_element_type=F32)


def _matmul_nt(a_bf, bt_bf, tm, tn):
    m, k = a_bf.shape
    n = bt_bf.shape[0]
    return pl.pallas_call(
        _mm_kernel,
        grid=(n // tn, m // tm),
        in_specs=[pl.BlockSpec((tm, k), lambda j, i: (i, 0)),
                  pl.BlockSpec((tn, k), lambda j, i: (j, 0))],
        out_specs=pl.BlockSpec((tm, tn), lambda j, i: (i, j)),
        out_shape=jax.ShapeDtypeStruct((m, n), F32),
        compiler_params=_params(("parallel", "parallel")),
        name="inproj",
    )(a_bf, bt_bf)


def _ln_swish_gate(y, gate, bdw, lng, lnb):
    y = y + bdw
    mu = jnp.mean(y, axis=-1, keepdims=True)
    yc = y - mu
    var = jnp.mean(yc * yc, axis=-1, keepdims=True)
    yn = yc * lax.rsqrt(var + EPS) * lng + lnb
    return _silu(yn) * _silu(gate)


def _conv_prompt_kernel(val_ref, glu_ref, gate_ref, valh_ref, gluh_ref, buf_ref, w_ref, bdw_ref,
                        lng_ref, lnb_ref, ya_ref, utail_ref, f_ref, g_ref, *, tt):
    ti = pl.program_id(1)
    u = val_ref[0] * _sigmoid(glu_ref[0])
    uh = valh_ref[0] * _sigmoid(gluh_ref[0])
    f_ref[0:HALO, :] = jnp.where(ti == 0, buf_ref[0], uh)
    f_ref[HALO:HALO + tt, :] = u
    nrows = g_ref.shape[1]
    for p in range(1, SUBLANES):
        g_ref[p - 1] = f_ref[pl.ds(p, nrows), :]
    off = HALO - (CONV_W - 1)
    acc = jnp.zeros((tt, D_A), F32)
    for j in range(CONV_W):
        p = (off + j) % SUBLANES
        a = off + j - p
        win = f_ref[pl.ds(a, tt), :] if p == 0 else g_ref[p - 1, pl.ds(a, tt), :]
        acc = acc + w_ref[j:j + 1, :] * win
    ya_ref[0] = _ln_swish_gate(acc, gate_ref[0], bdw_ref[...], lng_ref[...], lnb_ref[...])

    @pl.when(ti == pl.num_programs(1) - 1)
    def _():
        utail_ref[0] = f_ref[tt:tt + HALO, :]


def _conv_prompt(z3, buf32, w_dw, b_dw, ln_g, ln_b, tt):
    b, t, _ = z3.shape
    hb = tt // HALO
    cur = lambda name: pl.BlockSpec((1, tt, D_A), lambda bi, ti, c=_col(name, D_A): (bi, ti, c))
    halo = lambda name: pl.BlockSpec(
        (1, HALO, D_A), lambda bi, ti, c=_col(name, D_A): (bi, jnp.maximum(ti * hb - 1, 0), c))
    vec = pl.BlockSpec((1, D_A), lambda bi, ti: (0, 0))
    wpad = jnp.pad(w_dw, ((0, HALO - CONV_W), (0, 0)))
    return pl.pallas_call(
        functools.partial(_conv_prompt_kernel, tt=tt),
        grid=(b, t // tt),
        in_specs=[cur("a_val"), cur("a_glu"), cur("a_gate"), halo("a_val"), halo("a_glu"),
                  pl.BlockSpec((1, HALO, D_A), lambda bi, ti: (bi, 0, 0)),
                  pl.BlockSpec((HALO, D_A), lambda bi, ti: (0, 0)), vec, vec, vec],
        out_specs=[pl.BlockSpec((1, tt, D_A), lambda bi, ti: (bi, ti, 0)),
                   pl.BlockSpec((1, HALO, D_A), lambda bi, ti: (bi, 0, 0))],
        out_shape=[jax.ShapeDtypeStruct((b, t, D_A), F32),
                   jax.ShapeDtypeStruct((b, HALO, D_A), F32)],
        scratch_shapes=[pltpu.VMEM((HALO + tt, D_A), F32),
                        pltpu.VMEM((SUBLANES - 1, HALO + tt - SUBLANES, D_A), F32)],
        compiler_params=_params(("parallel", "arbitrary")),
        name="conv_prompt",
    )(z3, z3, z3, z3, z3, buf32, wpad, b_dw.reshape(1, D_A), ln_g.reshape(1, D_A), ln_b.reshape(1, D_A))


def _conv_sample_kernel(val_ref, glu_ref, gate_ref, buf_ref, w_ref, bdw_ref, lng_ref, lnb_ref,
                        ya_ref, u_ref, *, ts):
    nb = CONV_W - 1
    for t in range(ts):
        u_ref[t] = val_ref[t] * _sigmoid(glu_ref[t])
    for t in range(ts):
        acc = jnp.zeros(u_ref.shape[1:], F32)
        for j in range(CONV_W):
            r = t + j
            src = buf_ref[r] if r < nb else u_ref[r - nb]
            acc = acc + w_ref[j:j + 1, :] * src
        ya_ref[t] = _ln_swish_gate(acc, gate_ref[t], bdw_ref[...], lng_ref[...], lnb_ref[...])


def _conv_sample(val_t, glu_t, gate_t, buf_t, w_dw, b_dw, ln_g, ln_b):
    ts, b, _ = val_t.shape
    assert ts <= CONV_W - 1
    return pl.pallas_call(
        functools.partial(_conv_sample_kernel, ts=ts),
        out_shape=[jax.ShapeDtypeStruct((ts, b, D_A), F32), jax.ShapeDtypeStruct((ts, b, D_A), F32)],
        compiler_params=pltpu.CompilerParams(vmem_limit_bytes=VMEM_LIMIT),
        name="conv_sample",
    )(val_t, glu_t, gate_t, buf_t, w_dw, b_dw.reshape(1, D_A), ln_g.reshape(1, D_A), ln_b.reshape(1, D_A))


def _lane_iota(shape):
    return lax.broadcasted_iota(I32, shape, len(shape) - 1)


def _swap_half(x):
    w = x.shape[-1]
    half = HEAD_DIM // 2
    first = (_lane_iota(x.shape) % HEAD_DIM) < half
    return jnp.where(first, pltpu.roll(x, w - half, 1), pltpu.roll(x, half, 1))


def _group_sum(s):
    same_group = (lax.broadcasted_iota(I32, (LANES, LANES), 0) // HEAD_DIM
                  == lax.broadcasted_iota(I32, (LANES, LANES), 1) // HEAD_DIM).astype(BF16)
    hi = s.astype(BF16)
    r1 = s - hi.astype(F32)
    mid = r1.astype(BF16)
    lo = (r1 - mid.astype(F32)).astype(BF16)
    cols = []
    for c in range(s.shape[-1] // LANES):
        sl = slice(c * LANES, (c + 1) * LANES)
        cols.append(jnp.dot(hi[:, sl], same_group, preferred_element_type=F32)
                    + jnp.dot(mid[:, sl], same_group, preferred_element_type=F32)
                    + jnp.dot(lo[:, sl], same_group, preferred_element_type=F32))
    return cols[0] if len(cols) == 1 else jnp.concatenate(cols, axis=-1)


def _tile_lanes(t, w):
    reps = w // t.shape[-1]
    return t if reps == 1 else jnp.concatenate([t] * reps, axis=-1)


def _rope(x, cos, sin):
    w = x.shape[-1]
    return x * _tile_lanes(cos, w) + _swap_half(x) * _tile_lanes(sin, w)


def _head_rms(x, g):
    ms = _group_sum(x * x) * (1.0 / HEAD_DIM)
    y = x * lax.rsqrt(ms + EPS)
    return y if g is None else y * _tile_lanes(g, x.shape[-1])


def _qkrope_kernel(zq_ref, zk_ref, zqi_ref, zki_ref, zv_ref, cos_ref, sin_ref, qg_ref, kg_ref,
                   q_ref, k_ref, qi_ref, ki_ref, v_ref, kb_ref, vb_ref, kib_ref, *, feature_major):
    cos = cos_ref[...]
    sin = sin_ref[...]
    q_ref[...] = _rope(_head_rms(zq_ref[...], qg_ref[...]), cos, sin)
    k = _rope(_head_rms(zk_ref[...], kg_ref[...]), cos, sin)
    kb_ref[...] = k.astype(BF16)
    qi_ref[...] = _rope(zqi_ref[...], cos, sin)
    ki = _rope(_head_rms(zki_ref[...], None), cos, sin)
    kib_ref[...] = ki[:, :IDX_DIM].astype(BF16)
    v = zv_ref[...]
    vb_ref[...] = v.astype(BF16)
    if feature_major:
        k_ref[0] = k.T
        ki_ref[0] = ki.T[:IDX_DIM]
        v_ref[0] = v.T
    else:
        k_ref[...] = k
        ki_ref[...] = ki[:, :IDX_DIM]
        v_ref[...] = v


def _qkrope(z, cos_t, sin_t, qg, kg, tm, table_tiles, feature_major):
    m = z.shape[0]
    zc = lambda name, w: pl.BlockSpec((tm, w), lambda i, c=_col(name, w): (i, c))
    tab = pl.BlockSpec((tm, LANES), lambda i: (i % table_tiles, 0))
    vec = pl.BlockSpec((1, LANES), lambda i: (0, 0))
    row = lambda w: pl.BlockSpec((tm, w), lambda i: (i, 0))
    qi_w = IDX_HEADS * IDX_DIM
    if feature_major:
        nseq, t = m // (tm * table_tiles), tm * table_tiles
        fm = lambda w: pl.BlockSpec((1, w, tm), lambda i: (i // table_tiles, 0, i % table_tiles))
        f32_spec = lambda w: fm(w)
        f32_shape = lambda w: jax.ShapeDtypeStruct((nseq, w, t), F32)
    else:
        f32_spec = row
        f32_shape = lambda w: jax.ShapeDtypeStruct((m, w), F32)
    return pl.pallas_call(
        functools.partial(_qkrope_kernel, feature_major=feature_major),
        grid=(m // tm,),
        in_specs=[zc("zq", D_B), zc("zk", KV_DIM), zc("zqi", qi_w), zc("zki", LANES), zc("zv", KV_DIM),
                  tab, tab, vec, vec],
        out_specs=[row(D_B), f32_spec(KV_DIM), row(qi_w), f32_spec(IDX_DIM), f32_spec(KV_DIM),
                   row(KV_DIM), row(KV_DIM), row(IDX_DIM)],
        out_shape=[jax.ShapeDtypeStruct((m, D_B), F32), f32_shape(KV_DIM),
                   jax.ShapeDtypeStruct((m, qi_w), F32), f32_shape(IDX_DIM), f32_shape(KV_DIM),
                   jax.ShapeDtypeStruct((m, KV_DIM), BF16), jax.ShapeDtypeStruct((m, KV_DIM), BF16),
                   jax.ShapeDtypeStruct((m, IDX_DIM), BF16)],
        compiler_params=_params(("parallel",)),
        name="qkrope",
    )(z, z, z, z, z, cos_t, sin_t, qg, kg)


def _rope_tables(pos):
    half = HEAD_DIM // 2
    inv = ROPE_THETA ** (-jnp.arange(half, dtype=F32) / half)
    ang = pos.astype(F32)[:, None] * inv[None, :]
    cos = jnp.cos(ang)
    sin = jnp.sin(ang)
    cos64 = jnp.concatenate([cos, cos], axis=1)
    sin64 = jnp.concatenate([-sin, sin], axis=1)
    reps = LANES // HEAD_DIM
    return jnp.tile(cos64, (1, reps)), jnp.tile(sin64, (1, reps))


_SIGN = np.int32(-2 ** 31)
_MAG = np.int32(0x7FFFFFFF)
QK_SCALE = HEAD_DIM ** -0.5 * math.log2(math.e)
SEARCH_UNROLL = 4
SAMPLE_DIGIT_BITS = 3


def _dot_nt(a, b):
    return lax.dot_general(a, b, (((1,), (1,)), ((), ())), preferred_element_type=F32)


def _row_count(mask):
    return jnp.sum(mask.astype(F32), axis=1, keepdims=True)


def _ordered_to_float(t):
    key = t ^ _SIGN
    return lax.bitcast_convert_type(jnp.where(key < 0, key ^ _MAG, key), F32)


def _threshold_bits(sc_ref, kf):
    nq, nl = sc_ref.shape

    def cond(c):
        i, _, cnt_t = c
        return (i < 32) & (jnp.max(jnp.abs(cnt_t - kf)) > 0.0)

    def body(c):
        i, t, cnt_t = c
        for j in range(SEARCH_UNROLL):
            cand = t | jnp.left_shift(jnp.int32(1), 31 - (i + j))
            cnt = _row_count(sc_ref[...] >= _ordered_to_float(cand))
            ok = cnt >= kf
            t = jnp.where(ok, cand, t)
            cnt_t = jnp.where(ok, cnt, cnt_t)
        return i + SEARCH_UNROLL, t, cnt_t

    init = (jnp.int32(0), jnp.zeros((nq, 1), I32), jnp.full((nq, 1), float(nl), F32))
    return lax.while_loop(cond, body, init)[1:]


def _threshold_digits(sc_ref, kf, digit_bits):
    nq, nl = sc_ref.shape
    t = jnp.zeros((nq, 1), I32)
    cnt_t = jnp.full((nq, 1), float(nl), F32)
    pos = 32
    while pos > 0:
        nb = (pos % digit_bits) or digit_bits
        pos -= nb
        digit = jnp.zeros((nq, 1), I32)
        for j in range(1, 2 ** nb):
            cand = t | np.uint32(j << pos).astype(np.int32)
            cnt = _row_count(sc_ref[...] >= _ordered_to_float(cand))
            ok = cnt >= kf
            digit = digit + ok.astype(I32)
            cnt_t = jnp.where(ok, cnt, cnt_t)
        t = t | jnp.left_shift(digit, pos)
    return t, cnt_t


def _select_topk(sc_ref, sel_ref, topk, digit_bits=1):
    nq, nl = sc_ref.shape
    if nl == topk:
        sel_ref[...] = jnp.ones((nq, nl), F32)
        return
    kf = float(topk)
    t, cnt_t = (_threshold_bits(sc_ref, kf) if digit_bits == 1
                else _threshold_digits(sc_ref, kf, digit_bits))
    tau = jnp.where(t == 0, -jnp.inf, _ordered_to_float(t))
    sel_ref[...] = (sc_ref[...] >= tau).astype(F32)
    tie_rows = (cnt_t > kf) & (tau > NEG)

    @pl.when(jnp.max(tie_rows.astype(F32)) > 0.0)
    def _():
        need = kf - _row_count(sc_ref[...] > tau)
        tri = (lax.broadcasted_iota(I32, (LANES, LANES), 0)
               < lax.broadcasted_iota(I32, (LANES, LANES), 1)).astype(BF16)
        run = jnp.zeros((nq, 1), F32)
        for j in range(nl // LANES):
            sl = slice(j * LANES, (j + 1) * LANES)
            sj = sc_ref[:, sl]
            eqj = sj == tau
            rank = jnp.dot(eqj.astype(BF16), tri, preferred_element_type=F32) + run
            sel_ref[:, sl] = ((sj > tau) | (eqj & (rank < need))).astype(F32)
            run = run + _row_count(eqj)


def _dsa_core(q, qi, wi, bgate, qpos, kb_ref, vb_ref, kib_ref, sc_ref, sel_ref, topk):
    nq = q.shape[0]
    nl = kb_ref.shape[0]
    kib = kib_ref[...]
    qis = (qi * (IDX_DIM ** -0.5)).astype(BF16)
    wis = wi * (IDX_HEADS ** -0.5)
    scores = jnp.zeros((nq, nl), F32)
    for h in range(IDX_HEADS):
        s = _dot_nt(qis[:, h * IDX_DIM:(h + 1) * IDX_DIM], kib)
        scores = scores + jnp.maximum(s, 0.0) * wis[:, h:h + 1]
    causal = _lane_iota((nq, nl)) <= qpos
    sc_ref[...] = jnp.where(causal, scores, NEG)
    _select_topk(sc_ref, sel_ref, topk)

    sel = (sel_ref[...] > 0.0) & causal
    lane = _lane_iota((nq, LANES))
    group_w = N_HEADS // N_KV_HEADS
    kb = kb_ref[...]
    vb = vb_ref[...]
    v_lane = _lane_iota(vb.shape) // HEAD_DIM
    vones = [jnp.where(v_lane == g, vb, jnp.ones_like(vb)) for g in range(N_KV_HEADS)]
    qs = q * QK_SCALE
    outs = []
    for h in range(N_HEADS):
        g = h // group_w
        c = (h * HEAD_DIM) // LANES
        x = qs[:, c * LANES:(c + 1) * LANES]
        if (h % 2) != g:
            x = pltpu.roll(x, HEAD_DIM, 1)
        in_g = (lane // HEAD_DIM) == g
        xq = jnp.where(in_g, x, 0.0).astype(BF16)
        s = jnp.where(sel, _dot_nt(xq, kb), NEG)
        mx = jnp.max(s, axis=1, keepdims=True)
        p = jnp.exp2(s - mx)
        o = jnp.dot(p.astype(BF16), vones[g], preferred_element_type=F32)
        den_lane = ((g + 1) % N_KV_HEADS) * HEAD_DIM
        o = o / o[:, den_lane:den_lane + 1]
        o = jnp.where(in_g, o, 0.0)
        if (h % 2) != g:
            o = pltpu.roll(o, HEAD_DIM, 1)
        outs.append(o)
    cols = [outs[2 * c] + outs[2 * c + 1] for c in range(N_HEADS // 2)]
    return jnp.concatenate(cols, axis=1) * _silu(bgate)


def _dsa_prompt_kernel(acc_hbm, q_ref, qi_ref, wi_ref, bg_ref, kb_ref, vb_ref, kib_ref, o_ref, sc_ref, sel_ref,
                       *, topk, q_lo):
    del acc_hbm
    qb = pl.program_id(1) + q_lo
    nq = q_ref.shape[1]
    qpos = qb * nq + lax.broadcasted_iota(I32, (nq, 1), 0)
    o_ref[0] = _dsa_core(q_ref[0], qi_ref[0], wi_ref[0], bg_ref[0], qpos,
                         kb_ref.at[0], vb_ref.at[0], kib_ref.at[0], sc_ref, sel_ref, topk)


def _dsa_prompt_bucket(acc, q3, qi3, z3, kb3, vb3, kib3, nq, q_lo, q_hi):
    b, t, _ = q3.shape
    topk = min(TOPK_MAX, t // 4)
    nl = q_hi * nq
    qi_w = IDX_HEADS * IDX_DIM
    qblk = lambda w: pl.BlockSpec((1, nq, w), lambda bi, i: (bi, i + q_lo, 0))
    zblk = lambda name, w: pl.BlockSpec((1, nq, w), lambda bi, i, c=_col(name, w): (bi, i + q_lo, c))
    keys = lambda w: pl.BlockSpec((1, nl, w), lambda bi, i: (bi, 0, 0))
    return pl.pallas_call(
        functools.partial(_dsa_prompt_kernel, topk=topk, q_lo=q_lo),
        grid=(b, q_hi - q_lo),
        in_specs=[pl.BlockSpec(memory_space=pl.ANY),
                  qblk(D_B), qblk(qi_w), zblk("zwi", LANES), zblk("b_gate", D_B),
                  keys(KV_DIM), keys(KV_DIM), keys(IDX_DIM)],
        out_specs=pl.BlockSpec((1, nq, D_B), lambda bi, i: (bi, i + q_lo, 0)),
        out_shape=jax.ShapeDtypeStruct((b, t, D_B), F32),
        input_output_aliases={0: 0},
        scratch_shapes=[pltpu.VMEM((nq, nl), F32), pltpu.VMEM((nq, nl), F32)],
        compiler_params=_params(("parallel", "parallel")),
        name="dsa_prompt",
    )(acc, q3, qi3, z3, z3, kb3, vb3, kib3)


DSA_BUCKETS = 8
DSA_ROWS = 256


def _dsa_prompt(q3, qi3, z3, kb3, vb3, kib3):
    t = q3.shape[1]
    nq = DSA_ROWS if t % DSA_ROWS == 0 else Q_BLOCK
    nblk = t // nq
    step = max(1, nblk // DSA_BUCKETS)
    acc = jnp.zeros(q3.shape[:2] + (D_B,), F32)
    for lo in range(0, nblk, step):
        acc = _dsa_prompt_bucket(acc, q3, qi3, z3, kb3, vb3, kib3, nq, lo, min(lo + step, nblk))
    return acc


def _dsa_sample_kernel(pt_ref, q_ref, qi_ref, wi_ref, bg_ref, knt_ref, vnt_ref, kint_ref,
                       ck_hbm, cv_hbm, cki_hbm, o_ref, kt_ref, vt_ref, kit_ref, sc_ref, sel_ref, sems,
                       *, topk, n_pages, page_base, ts):
    bi = pl.program_id(0)
    nb = pl.num_programs(0)
    nq = q_ref.shape[1]
    past = n_pages * PAGE_SIZE
    nl = kt_ref.shape[2]
    group_w = N_HEADS // N_KV_HEADS
    slot = bi % 2

    def copies(seq, s, p):
        page = pt_ref[seq, p] + page_base
        dst = pl.ds(p * PAGE_SIZE, PAGE_SIZE)
        return (pltpu.make_async_copy(ck_hbm.at[page], kt_ref.at[s, :, dst], sems.at[s, 0]),
                pltpu.make_async_copy(cv_hbm.at[page], vt_ref.at[s, :, dst], sems.at[s, 1]),
                pltpu.make_async_copy(cki_hbm.at[page], kit_ref.at[s, :, dst], sems.at[s, 2]))

    def start_gather(seq, s):
        for p in range(n_pages):
            for cp in copies(seq, s, p):
                cp.start()

    @pl.when(bi == 0)
    def _():
        start_gather(0, 0)

    @pl.when(bi + 1 < nb)
    def _():
        start_gather(bi + 1, 1 - slot)

    kt_ref[slot, :, past:nl] = knt_ref[0]
    vt_ref[slot, :, past:nl] = vnt_ref[0]
    kit_ref[slot, :, past:nl] = kint_ref[0]

    for p in range(n_pages):
        for cp in copies(bi, slot, p):
            cp.wait()
    kt_ref, vt_ref, kit_ref = kt_ref.at[slot], vt_ref.at[slot], kit_ref.at[slot]

    row = lax.broadcasted_iota(I32, (nq, 1), 0)
    qpos = past + jnp.minimum(row, ts - 1)
    stack = lambda x, w, heads: jnp.concatenate([x[:, h * w:(h + 1) * w] for h in heads], axis=0)

    qis = stack(qi_ref[0] * (IDX_DIM ** -0.5), IDX_DIM, range(IDX_HEADS)).astype(BF16)
    s = jnp.dot(qis, kit_ref[...].astype(BF16), preferred_element_type=F32)
    wis = wi_ref[0] * (IDX_HEADS ** -0.5)
    scores = jnp.zeros((nq, nl), F32)
    for h in range(IDX_HEADS):
        scores = scores + jnp.maximum(s[h * nq:(h + 1) * nq], 0.0) * wis[:, h:h + 1]
    causal = _lane_iota((nq, nl)) <= qpos
    sc_ref[...] = jnp.where(causal, scores, NEG)
    _select_topk(sc_ref, sel_ref, topk, digit_bits=SAMPLE_DIGIT_BITS)

    sel = (sel_ref[...] > 0.0) & causal
    sel_g = jnp.concatenate([sel] * group_w, axis=0)
    qs = q_ref[0] * QK_SCALE
    outs = []
    for g in range(N_KV_HEADS):
        rows = slice(g * HEAD_DIM, (g + 1) * HEAD_DIM)
        qg = stack(qs, HEAD_DIM, range(g * group_w, (g + 1) * group_w)).astype(BF16)
        sg = jnp.dot(qg, kt_ref[rows, :].astype(BF16), preferred_element_type=F32)
        sg = jnp.where(sel_g, sg, NEG)
        mx = jnp.max(sg, axis=1, keepdims=True)
        p = jnp.exp2(sg - mx)
        den = jnp.sum(p, axis=1, keepdims=True)
        og = _dot_nt(p.astype(BF16), vt_ref[rows, :].astype(BF16)) / den
        outs += [og[j * nq:(j + 1) * nq] for j in range(group_w)]
    o_ref[0] = jnp.concatenate(outs, axis=1) * _silu(bg_ref[0])


def _dsa_sample(page_table, q8, qi8, wi8, bg8, knt, vnt, kint, ckt, cvt, ckit, layer, depth, ts):
    b, nq, _ = q8.shape
    n_pages = page_table.shape[1]
    past = n_pages * PAGE_SIZE
    nl = past + LANES
    topk = min(TOPK_MAX, (past + ts) // 4)
    n_pool = ckt.shape[0] // depth
    qi_w = IDX_HEADS * IDX_DIM
    blk = lambda r, w: pl.BlockSpec((1, r, w), lambda bi, pt: (bi, 0, 0))
    anyspec = pl.BlockSpec(memory_space=pl.ANY)
    grid_spec = pltpu.PrefetchScalarGridSpec(
        num_scalar_prefetch=1,
        grid=(b,),
        in_specs=[blk(nq, D_B), blk(nq, qi_w), blk(nq, LANES), blk(nq, D_B),
                  blk(KV_DIM, LANES), blk(KV_DIM, LANES), blk(IDX_DIM, LANES),
                  anyspec, anyspec, anyspec],
        out_specs=pl.BlockSpec((1, nq, D_B), lambda bi, pt: (bi, 0, 0)),
        scratch_shapes=[pltpu.VMEM((2, KV_DIM, nl), F32), pltpu.VMEM((2, KV_DIM, nl), F32),
                        pltpu.VMEM((2, IDX_DIM, nl), F32),
                        pltpu.VMEM((nq, nl), F32), pltpu.VMEM((nq, nl), F32),
                        pltpu.SemaphoreType.DMA((2, 3))])
    return pl.pallas_call(
        functools.partial(_dsa_sample_kernel, topk=topk, n_pages=n_pages, page_base=layer * n_pool, ts=ts),
        grid_spec=grid_spec,
        out_shape=jax.ShapeDtypeStruct((b, nq, D_B), F32),
        compiler_params=_params(("arbitrary",)),
        name="dsa_sample",
    )(page_table, q8, qi8, wi8, bg8, knt, vnt, kint, ckt, cvt, ckit)


def _hgrn_kernel(cq_ref, cf_ref, ci_ref, cg_ref, lb_ref, ng_ref, s0_ref, yc_ref, s_ref, *, c, valid):
    ci_idx = pl.program_id(1)

    @pl.when(ci_idx == 0)
    def _():
        s_ref[...] = s0_ref[...]

    for r in range(cq_ref.shape[0]):
        _hgrn_chunk(r, cq_ref, cf_ref, ci_ref, cg_ref, lb_ref, ng_ref, yc_ref, s_ref, c, valid)


def _hgrn_chunk(r, cq_ref, cf_ref, ci_ref, cg_ref, lb_ref, ng_ref, yc_ref, s_ref, c, valid):
    fx = cf_ref[r]
    lb = lb_ref[...]
    log_f = jnp.log(jnp.maximum(lb, LB_FLOOR) + (1.0 - lb) * _sigmoid(fx))
    kk = (1.0 - lb) * _sigmoid(-fx)
    qc = _silu(cq_ref[r])
    iv = ci_ref[r]
    row = lax.broadcasted_iota(I32, (c, 1), 0)
    if valid < c:
        log_f = jnp.where(row < valid, log_f, 0.0)

    rr = lax.broadcasted_iota(I32, (c, c), 0)
    cc = lax.broadcasted_iota(I32, (c, c), 1)
    hs = [slice(h * C_KDIM, (h + 1) * C_KDIM) for h in range(C_HEADS)]
    att = [jnp.zeros((c, c), F32) for _ in range(C_HEADS)]
    cs = log_f
    tot = log_f
    m = 1
    while m < c:
        right = ((row // m) % 2) == 1
        qm = jnp.where(right, qc * jnp.exp(cs), 0.0).astype(BF16)
        km = jnp.where(right, 0.0, kk * jnp.exp(tot - cs)).astype(BF16)
        pair = (rr // (2 * m)) == (cc // (2 * m))
        for h in range(C_HEADS):
            att[h] = att[h] + jnp.where(pair, _dot_nt(qm[:, hs[h]], km[:, hs[h]]), 0.0)
        sib = jnp.where(right, pltpu.roll(tot, m, 0), pltpu.roll(tot, c - m, 0))
        cs = jnp.where(right, cs + sib, cs)
        tot = tot + sib
        m *= 2
    qdec = (qc * jnp.exp(cs)).astype(BF16)
    kdec = (kk * jnp.exp(tot - cs)).astype(BF16)
    ivb = iv.astype(BF16)
    eye_c = rr == cc
    eye_k = (lax.broadcasted_iota(I32, (C_KDIM, C_KDIM), 0)
             == lax.broadcasted_iota(I32, (C_KDIM, C_KDIM), 1))
    ys = []
    for h in range(C_HEADS):
        sl = hs[h]
        diag = jnp.sum(qc[:, sl] * kk[:, sl], axis=1, keepdims=True)
        a_h = att[h] + jnp.where(eye_c, diag, 0.0)
        s_h = s_ref[r, h]
        o = (jnp.dot(a_h.astype(BF16), ivb[:, sl], preferred_element_type=F32)
             + jnp.dot(qdec[:, sl], s_h.astype(BF16), preferred_element_type=F32))
        e_end = jnp.exp(tot[0:1, sl])
        e_col = jnp.sum(jnp.where(eye_k, e_end, 0.0), axis=1, keepdims=True)
        upd = lax.dot_general(kdec[:, sl], ivb[:, sl], (((0,), (0,)), ((), ())),
                              preferred_element_type=F32)
        s_ref[r, h] = e_col * s_h + upd
        ms = jnp.mean(o * o, axis=-1, keepdims=True)
        ys.append(o * lax.rsqrt(ms + EPS) * ng_ref[...])
    yc_ref[r] = jnp.concatenate(ys, axis=1) * _silu(cg_ref[r])


HGRN_SEQS = 4


def _hgrn(z3, lb, cng, s0, c, valid):
    b, t, _ = z3.shape
    nb = math.gcd(b, HGRN_SEQS)
    zblk = lambda name: pl.BlockSpec((nb, c, D_C), lambda bi, i, col=_col(name, D_C): (bi, i, col))
    sblk = pl.BlockSpec((nb, C_HEADS, C_KDIM, C_VDIM), lambda bi, i: (bi, 0, 0, 0))
    return pl.pallas_call(
        functools.partial(_hgrn_kernel, c=c, valid=valid),
        grid=(b // nb, t // c),
        in_specs=[zblk("cq"), zblk("cf"), zblk("ci"), zblk("c_gate"),
                  pl.BlockSpec((1, C_FDIM), lambda bi, i: (0, 0)),
                  pl.BlockSpec((1, C_VDIM), lambda bi, i: (0, 0)), sblk],
        out_specs=[pl.BlockSpec((nb, c, D_C), lambda bi, i: (bi, i, 0)), sblk],
        out_shape=[jax.ShapeDtypeStruct((b, t, D_C), F32),
                   jax.ShapeDtypeStruct((b, C_HEADS, C_KDIM, C_VDIM), F32)],
        compiler_params=_params(("parallel", "arbitrary")),
        name="hgrn",
    )(z3, z3, z3, z3, lb.reshape(1, C_FDIM), cng.reshape(1, C_VDIM), s0)


def _merge_kernel(x_ref, ya_ref, yb_ref, yc_ref, ga_ref, gb_ref, gc_ref, gate_ref,
                  wa_ref, wb_ref, wc_ref, wo_ref, o_ref):
    def proj(y_ref, w_ref):
        return jnp.dot(y_ref[...].astype(BF16), w_ref[...], preferred_element_type=F32)

    m = (_sigmoid(ga_ref[...]) * proj(ya_ref, wa_ref)
         + _sigmoid(gb_ref[...]) * proj(yb_ref, wb_ref)
         + _sigmoid(gc_ref[...]) * proj(yc_ref, wc_ref))
    o_ref[...] = x_ref[...] + gate_ref[0] * jnp.dot(m.astype(BF16), wo_ref[...], preferred_element_type=F32)


def _merge(x2, ya, yb, yc, z, gate3, wa, wb, wc, wo, tm, tiles_per_mod):
    m, d = x2.shape
    r = gate3.shape[1]
    row = lambda w: pl.BlockSpec((tm, w), lambda i: (i, 0))
    zc = lambda name: pl.BlockSpec((tm, d), lambda i, c=_col(name, d): (i, c))
    wspec = lambda k: pl.BlockSpec((k, d), lambda i: (0, 0))
    return pl.pallas_call(
        _merge_kernel,
        grid=(m // tm,),
        in_specs=[row(d), row(D_A), row(D_B), row(D_C), zc("g_a"), zc("g_b"), zc("g_c"),
                  pl.BlockSpec((1, r, d), lambda i: (i // tiles_per_mod, 0, 0)),
                  wspec(D_A), wspec(D_B), wspec(D_C), wspec(d)],
        out_specs=row(d),
        out_shape=jax.ShapeDtypeStruct((m, d), F32),
        compiler_params=_params(("parallel",)),
        name="merge",
    )(x2, ya, yb, yc, z, z, z, gate3, wa, wb, wc, wo)


def _pick_tile(n, pref):
    t = min(pref, n)
    while n % t:
        t //= 2
    return t


def kernel(x_prompt, x_sample, cache_k, cache_v, cache_idx_k, state_conv, state_hgrn, page_table,
           c_prompt, c_sample, w_ada, b_ada, norm_g, w_in, w_dw, b_dw, ln_g, ln_b, q_norm_g, k_norm_g,
           lb_logits, c_norm_g, w_proj_a, w_proj_b, w_proj_c, w_out):
    depth = w_in.shape[0]
    bp, tp, d = x_prompt.shape
    bs, ts, _ = x_sample.shape
    mp, ms = bp * tp, bs * ts
    n_pages = page_table.shape[1]
    past = n_pages * PAGE_SIZE
    assert d == D_MODEL and tp % Q_BLOCK == 0 and tp % CHUNK == 0 and tp >= HALO
    assert ts <= SUBLANES and ms % SUBLANES == 0

    lbp = jax.nn.softmax(lb_logits.astype(F32), axis=0)
    lb_all = jnp.cumsum(lbp, axis=0) - lbp[0:1]
    cos_p, sin_p = _rope_tables(jnp.arange(tp))
    cos_s, sin_s = _rope_tables(past + (jnp.arange(ms) % ts))
    ckt = jnp.transpose(cache_k, (0, 1, 3, 4, 2)).reshape(-1, KV_DIM, PAGE_SIZE)
    cvt = jnp.transpose(cache_v, (0, 1, 3, 4, 2)).reshape(-1, KV_DIM, PAGE_SIZE)
    ckit = jnp.transpose(cache_idx_k, (0, 1, 3, 2)).reshape(-1, IDX_DIM, PAGE_SIZE)
    c_all = jnp.concatenate([c_prompt, c_sample], axis=0)

    tm_p = _pick_tile(mp, 1024)
    tm_p = _pick_tile(tp, tm_p)
    tm_in = _pick_tile(tp, 2048)
    tt = _pick_tile(tp, 512)
    tm_g = _pick_tile(tp, 512)
    zeros_buf = jnp.zeros((bp, HALO, D_A), F32)
    s00 = jnp.zeros((bp, C_HEADS, C_KDIM, C_VDIM), F32)
    pad_rows = lambda a3: jnp.pad(a3, ((0, 0), (0, SUBLANES - ts), (0, 0)))

    xp = x_prompt.reshape(mp, d)
    xs = x_sample.reshape(ms, d)
    outs = [[] for _ in range(10)]
    for l in range(depth):
        mod = _ada(c_all, w_ada[l].astype(BF16), b_ada[l])
        shift, scale, gate = mod[:, :d], mod[:, d:2 * d], mod[:, 2 * d:]
        w_in_t = _pack_w_in_t(w_in[l]).astype(BF16)
        wa, wb, wc, wo = (w_proj_a[l].astype(BF16), w_proj_b[l].astype(BF16),
                          w_proj_c[l].astype(BF16), w_out[l].astype(BF16))
        qg = jnp.tile(q_norm_g[l], LANES // HEAD_DIM).reshape(1, LANES)
        kg = jnp.tile(k_norm_g[l], LANES // HEAD_DIM).reshape(1, LANES)
        lb = lb_all[l]

        per_seq = lambda a: a[:bp].reshape(bp, 1, d)
        h = _prenorm(xp, norm_g[l], per_seq(scale), per_seq(shift), tm_p, tp // tm_p)
        z = _matmul_nt(h, w_in_t, tm_in, NZ // 6)
        z3 = z.reshape(bp, tp, NZ)
        ya, utail = _conv_prompt(z3, zeros_buf, w_dw[l], b_dw[l], ln_g[l], ln_b[l], tt)
        q, kt, qi, kit, vt, kb, vb, kib = _qkrope(z, cos_p, sin_p, qg, kg, tm_p, tp // tm_p, True)
        r3 = lambda a: a.reshape(bp, tp, a.shape[-1])
        yb = _dsa_prompt(r3(q), r3(qi), z3, r3(kb), r3(vb), r3(kib))
        yc, s_p = _hgrn(z3, lb, c_norm_g[l], s00, CHUNK, CHUNK)
        xp = _merge(xp, ya.reshape(mp, D_A), yb.reshape(mp, D_B), yc.reshape(mp, D_C), z,
                    per_seq(gate), wa, wb, wc, wo, tm_g, tp // tm_g)
        heads_last = lambda a: jnp.transpose(a.reshape(bp, N_KV_HEADS, HEAD_DIM, tp), (0, 3, 1, 2))
        outs[0].append(heads_last(kt))
        outs[1].append(heads_last(vt))
        outs[2].append(jnp.swapaxes(kit, 1, 2))
        outs[3].append(utail[:, HALO - (CONV_W - 1):, :])
        outs[4].append(s_p)

        per_row = lambda a: jnp.repeat(a[bp:], ts, axis=0).reshape(1, ms, d)
        hs = _prenorm(xs, norm_g[l], per_row(scale), per_row(shift), ms, 1)
        zs = _matmul_nt(hs, w_in_t, ms, NZ // 6)
        zs3 = zs.reshape(bs, ts, NZ)
        seg_t = lambda name: jnp.swapaxes(
            zs3[:, :, _PACK_OFF[name]:_PACK_OFF[name] + D_A], 0, 1)
        ya_t, u_t = _conv_sample(seg_t("a_val"), seg_t("a_glu"), seg_t("a_gate"),
                                 jnp.swapaxes(state_conv[l], 0, 1), w_dw[l], b_dw[l], ln_g[l], ln_b[l])
        ya_s = jnp.swapaxes(ya_t, 0, 1).reshape(ms, D_A)
        q, k, qi, ki, v, _, _, _ = _qkrope(zs, cos_s, sin_s, qg, kg, ms, 1, False)
        p8 = lambda a: pad_rows(a.reshape(bs, ts, a.shape[-1]))
        zs8 = pad_rows(zs3)
        seg8 = lambda name, w: zs8[:, :, _PACK_OFF[name]:_PACK_OFF[name] + w]
        new_t = lambda a: jnp.pad(jnp.swapaxes(a.reshape(bs, ts, a.shape[-1]), 1, 2),
                                  ((0, 0), (0, 0), (0, LANES - ts)))
        yb8 = _dsa_sample(page_table, p8(q), p8(qi), seg8("zwi", LANES), seg8("b_gate", D_B),
                          new_t(k), new_t(v), new_t(ki), ckt, cvt, ckit, l, depth, ts)
        yc8, s_s = _hgrn(zs8, lb, c_norm_g[l], state_hgrn[l], SUBLANES, ts)
        xs = _merge(xs, ya_s, yb8[:, :ts].reshape(ms, D_B), yc8[:, :ts].reshape(ms, D_C), zs,
                    per_row(gate), wa, wb, wc, wo, ms, 1)
        outs[5].append(k.reshape(bs, ts, N_KV_HEADS, HEAD_DIM))
        outs[6].append(v.reshape(bs, ts, N_KV_HEADS, HEAD_DIM))
        outs[7].append(ki.reshape(bs, ts, IDX_DIM))
        outs[8].append(jnp.concatenate([state_conv[l][:, ts:], jnp.swapaxes(u_t, 0, 1)], axis=1))
        outs[9].append(s_s)

    st = [jnp.stack(o) for o in outs]
    return (xp.reshape(bp, tp, d), xs.reshape(bs, ts, d), st[0], st[1], st[2], st[3], st[4],
            st[5], st[6], st[7], st[8], st[9])
```

```python
import functools
import math

import jax
import jax.numpy as jnp
import numpy as np
from jax import lax
from jax.experimental import pallas as pl
from jax.experimental.pallas import tpu as pltpu

F32 = jnp.float32
BF16 = jnp.bfloat16
I32 = jnp.int32

LANES = 128
SUBLANES = 8
VMEM_LIMIT = 56 * 1024 * 1024

D_MODEL = 1024
EPS = 1e-6
NEG = -1e30
LB_FLOOR = 1e-30
D_A = D_MODEL // 2
CONV_W = 31
HALO = 32
N_HEADS = 8
N_KV_HEADS = 2
HEAD_DIM = 64
D_B = N_HEADS * HEAD_DIM
KV_DIM = N_KV_HEADS * HEAD_DIM
IDX_HEADS = 4
IDX_DIM = 64
TOPK_MAX = 256
ROPE_THETA = 10000.0
Q_BLOCK = 128
C_HEADS = 4
C_KDIM = 128
C_VDIM = D_MODEL // 2 // C_HEADS
D_C = C_HEADS * C_VDIM
C_FDIM = C_HEADS * C_KDIM
CHUNK = 64
PAGE_SIZE = 128

_IN_NAMES = ("a_val", "a_glu", "a_gate", "zq", "zk", "zv", "zqi", "zki", "zwi", "b_gate",
             "cq", "cf", "ci", "c_gate", "g_a", "g_b", "g_c")
_IN_WIDTHS = (D_A, D_A, D_A, D_B, KV_DIM, KV_DIM, IDX_HEADS * IDX_DIM, IDX_DIM, IDX_HEADS, D_B,
              C_FDIM, C_FDIM, D_C, D_C, D_MODEL, D_MODEL, D_MODEL)
_IN_OFFS = dict(zip(_IN_NAMES, np.concatenate([[0], np.cumsum(_IN_WIDTHS)[:-1]]).tolist()))
_IN_W = dict(zip(_IN_NAMES, _IN_WIDTHS))
_PACK_ORDER = ("g_a", "g_b", "g_c", "a_val", "a_glu", "a_gate", "zq", "b_gate", "cq", "cf", "ci",
               "c_gate", "zqi", "zk", "zv", "zki", "zwi")


def _round_up(n, m):
    return (n + m - 1) // m * m


_PACK_W = {n: _round_up(_IN_W[n], LANES) for n in _PACK_ORDER}
_PACK_OFF = {}
_o = 0
for _n in _PACK_ORDER:
    assert _o % _PACK_W[_n] == 0
    _PACK_OFF[_n] = _o
    _o += _PACK_W[_n]
NZ = _o


def _pack_w_in_t(w):
    wt = jnp.swapaxes(w, 0, 1)
    parts = []
    for n in _PACK_ORDER:
        seg = wt[_IN_OFFS[n]:_IN_OFFS[n] + _IN_W[n]]
        pad = _PACK_W[n] - _IN_W[n]
        if pad:
            seg = jnp.pad(seg, ((0, pad), (0, 0)))
        parts.append(seg)
    return jnp.concatenate(parts, axis=0)


def _col(name, width):
    assert _PACK_OFF[name] % width == 0
    return _PACK_OFF[name] // width


def _sigmoid(x):
    return jax.nn.sigmoid(x)


def _silu(x):
    return x * jax.nn.sigmoid(x)


def _params(sem):
    return pltpu.CompilerParams(dimension_semantics=sem, vmem_limit_bytes=VMEM_LIMIT)


def _ada_kernel(c_ref, w_ref, b_ref, o_ref):
    c = c_ref[...]
    o_ref[...] = jnp.dot(_silu(c).astype(BF16), w_ref[...], preferred_element_type=F32) + b_ref[...]


def _ada(c, w_bf, b):
    n, d = c.shape
    nout = w_bf.shape[1]
    tn = D_MODEL
    return pl.pallas_call(
        _ada_kernel,
        grid=(nout // tn,),
        in_specs=[pl.BlockSpec((n, d), lambda j: (0, 0)),
                  pl.BlockSpec((d, tn), lambda j: (0, j)),
                  pl.BlockSpec((1, tn), lambda j: (0, j))],
        out_specs=pl.BlockSpec((n, tn), lambda j: (0, j)),
        out_shape=jax.ShapeDtypeStruct((n, nout), F32),
        compiler_params=_params(("arbitrary",)),
        name="ada",
    )(c, w_bf, b.reshape(1, nout))


def _prenorm_kernel(x_ref, g_ref, sc_ref, sh_ref, o_ref):
    x = x_ref[...]
    ms = jnp.mean(x * x, axis=-1, keepdims=True)
    y = x * lax.rsqrt(ms + EPS) * g_ref[...]
    o_ref[...] = (y * (1.0 + sc_ref[0]) + sh_ref[0]).astype(o_ref.dtype)


def _prenorm(x2, g, scale3, shift3, tm, tiles_per_mod):
    m, d = x2.shape
    r = scale3.shape[1]
    return pl.pallas_call(
        _prenorm_kernel,
        grid=(m // tm,),
        in_specs=[pl.BlockSpec((tm, d), lambda i: (i, 0)),
                  pl.BlockSpec((1, d), lambda i: (0, 0)),
                  pl.BlockSpec((1, r, d), lambda i: (i // tiles_per_mod, 0, 0)),
                  pl.BlockSpec((1, r, d), lambda i: (i // tiles_per_mod, 0, 0))],
        out_specs=pl.BlockSpec((tm, d), lambda i: (i, 0)),
        out_shape=jax.ShapeDtypeStruct((m, d), BF16),
        compiler_params=_params(("parallel",)),
        name="prenorm",
    )(x2, g.reshape(1, d), scale3, shift3)


def _mm_kernel(a_ref, bt_ref, o_ref):
    o_ref[...] = lax.dot_general(a_ref[...], bt_ref[...], (((1,), (1,)), ((), ())),
                                 preferred_element_type=F32)


def _matmul_nt(a_bf, bt_bf, tm, tn):
    m, k = a_bf.shape
    n = bt_bf.shape[0]
    return pl.pallas_call(
        _mm_kernel,
        grid=(n // tn, m // tm),
        in_specs=[pl.BlockSpec((tm, k), lambda j, i: (i, 0)),
                  pl.BlockSpec((tn, k), lambda j, i: (j, 0))],
        out_specs=pl.BlockSpec((tm, tn), lambda j, i: (i, j)),
        out_shape=jax.ShapeDtypeStruct((m, n), F32),
        compiler_params=_params(("parallel", "parallel")),
        name="inproj",
    )(a_bf, bt_bf)


def _ln_swish_gate(y, gate, bdw, lng, lnb):
    y = y + bdw
    mu = jnp.mean(y, axis=-1, keepdims=True)
    yc = y - mu
    var = jnp.mean(yc * yc, axis=-1, keepdims=True)
    yn = yc * lax.rsqrt(var + EPS) * lng + lnb
    return _silu(yn) * _silu(gate)


def _conv_prompt_kernel(val_ref, glu_ref, gate_ref, valh_ref, gluh_ref, buf_ref, w_ref, bdw_ref,
                        lng_ref, lnb_ref, ya_ref, utail_ref, f_ref, g_ref, *, tt):
    ti = pl.program_id(1)
    u = val_ref[0] * _sigmoid(glu_ref[0])
    uh = valh_ref[0] * _sigmoid(gluh_ref[0])
    f_ref[0:HALO, :] = jnp.where(ti == 0, buf_ref[0], uh)
    f_ref[HALO:HALO + tt, :] = u
    nrows = g_ref.shape[1]
    for p in range(1, SUBLANES):
        g_ref[p - 1] = f_ref[pl.ds(p, nrows), :]
    off = HALO - (CONV_W - 1)
    acc = jnp.zeros((tt, D_A), F32)
    for j in range(CONV_W):
        p = (off + j) % SUBLANES
        a = off + j - p
        win = f_ref[pl.ds(a, tt), :] if p == 0 else g_ref[p - 1, pl.ds(a, tt), :]
        acc = acc + w_ref[j:j + 1, :] * win
    ya_ref[0] = _ln_swish_gate(acc, gate_ref[0], bdw_ref[...], lng_ref[...], lnb_ref[...])

    @pl.when(ti == pl.num_programs(1) - 1)
    def _():
        utail_ref[0] = f_ref[tt:tt + HALO, :]


def _conv_prompt(z3, buf32, w_dw, b_dw, ln_g, ln_b, tt):
    b, t, _ = z3.shape
    hb = tt // HALO
    cur = lambda name: pl.BlockSpec((1, tt, D_A), lambda bi, ti, c=_col(name, D_A): (bi, ti, c))
    halo = lambda name: pl.BlockSpec(
        (1, HALO, D_A), lambda bi, ti, c=_col(name, D_A): (bi, jnp.maximum(ti * hb - 1, 0), c))
    vec = pl.BlockSpec((1, D_A), lambda bi, ti: (0, 0))
    wpad = jnp.pad(w_dw, ((0, HALO - CONV_W), (0, 0)))
    return pl.pallas_call(
        functools.partial(_conv_prompt_kernel, tt=tt),
        grid=(b, t // tt),
        in_specs=[cur("a_val"), cur("a_glu"), cur("a_gate"), halo("a_val"), halo("a_glu"),
                  pl.BlockSpec((1, HALO, D_A), lambda bi, ti: (bi, 0, 0)),
                  pl.BlockSpec((HALO, D_A), lambda bi, ti: (0, 0)), vec, vec, vec],
        out_specs=[pl.BlockSpec((1, tt, D_A), lambda bi, ti: (bi, ti, 0)),
                   pl.BlockSpec((1, HALO, D_A), lambda bi, ti: (bi, 0, 0))],
        out_shape=[jax.ShapeDtypeStruct((b, t, D_A), F32),
                   jax.ShapeDtypeStruct((b, HALO, D_A), F32)],
        scratch_shapes=[pltpu.VMEM((HALO + tt, D_A), F32),
                        pltpu.VMEM((SUBLANES - 1, HALO + tt - SUBLANES, D_A), F32)],
        compiler_params=_params(("parallel", "arbitrary")),
        name="conv_prompt",
    )(z3, z3, z3, z3, z3, buf32, wpad, b_dw.reshape(1, D_A), ln_g.reshape(1, D_A), ln_b.reshape(1, D_A))


def _conv_sample_kernel(val_ref, glu_ref, gate_ref, buf_ref, w_ref, bdw_ref, lng_ref, lnb_ref,
                        ya_ref, u_ref, *, ts):
    nb = CONV_W - 1
    for t in range(ts):
        u_ref[t] = val_ref[t] * _sigmoid(glu_ref[t])
    for t in range(ts):
        acc = jnp.zeros(u_ref.shape[1:], F32)
        for j in range(CONV_W):
            r = t + j
            src = buf_ref[r] if r < nb else u_ref[r - nb]
            acc = acc + w_ref[j:j + 1, :] * src
        ya_ref[t] = _ln_swish_gate(acc, gate_ref[t], bdw_ref[...], lng_ref[...], lnb_ref[...])


def _conv_sample(val_t, glu_t, gate_t, buf_t, w_dw, b_dw, ln_g, ln_b):
    ts, b, _ = val_t.shape
    assert ts <= CONV_W - 1
    return pl.pallas_call(
        functools.partial(_conv_sample_kernel, ts=ts),
        out_shape=[jax.ShapeDtypeStruct((ts, b, D_A), F32), jax.ShapeDtypeStruct((ts, b, D_A), F32)],
        compiler_params=pltpu.CompilerParams(vmem_limit_bytes=VMEM_LIMIT),
        name="conv_sample",
    )(val_t, glu_t, gate_t, buf_t, w_dw, b_dw.reshape(1, D_A), ln_g.reshape(1, D_A), ln_b.reshape(1, D_A))


def _lane_iota(shape):
    return lax.broadcasted_iota(I32, shape, len(shape) - 1)


def _swap_half(x):
    w = x.shape[-1]
    half = HEAD_DIM // 2
    first = (_lane_iota(x.shape) % HEAD_DIM) < half
    return jnp.where(first, pltpu.roll(x, w - half, 1), pltpu.roll(x, half, 1))


def _group_sum(s):
    same_group = (lax.broadcasted_iota(I32, (LANES, LANES), 0) // HEAD_DIM
                  == lax.broadcasted_iota(I32, (LANES, LANES), 1) // HEAD_DIM).astype(BF16)
    hi = s.astype(BF16)
    r1 = s - hi.astype(F32)
    mid = r1.astype(BF16)
    lo = (r1 - mid.astype(F32)).astype(BF16)
    cols = []
    for c in range(s.shape[-1] // LANES):
        sl = slice(c * LANES, (c + 1) * LANES)
        cols.append(jnp.dot(hi[:, sl], same_group, preferred_element_type=F32)
                    + jnp.dot(mid[:, sl], same_group, preferred_element_type=F32)
                    + jnp.dot(lo[:, sl], same_group, preferred_element_type=F32))
    return cols[0] if len(cols) == 1 else jnp.concatenate(cols, axis=-1)


def _tile_lanes(t, w):
    reps = w // t.shape[-1]
    return t if reps == 1 else jnp.concatenate([t] * reps, axis=-1)


def _rope(x, cos, sin):
    w = x.shape[-1]
    return x * _tile_lanes(cos, w) + _swap_half(x) * _tile_lanes(sin, w)


def _head_rms(x, g):
    ms = _group_sum(x * x) * (1.0 / HEAD_DIM)
    y = x * lax.rsqrt(ms + EPS)
    return y if g is None else y * _tile_lanes(g, x.shape[-1])


def _qkrope_kernel(zq_ref, zk_ref, zqi_ref, zki_ref, zv_ref, cos_ref, sin_ref, qg_ref, kg_ref,
                   q_ref, k_ref, qi_ref, ki_ref, v_ref, kb_ref, vb_ref, kib_ref, *, feature_major):
    cos = cos_ref[...]
    sin = sin_ref[...]
    q_ref[...] = _rope(_head_rms(zq_ref[...], qg_ref[...]), cos, sin)
    k = _rope(_head_rms(zk_ref[...], kg_ref[...]), cos, sin)
    kb_ref[...] = k.astype(BF16)
    qi_ref[...] = _rope(zqi_ref[...], cos, sin)
    ki = _rope(_head_rms(zki_ref[...], None), cos, sin)
    kib_ref[...] = ki[:, :IDX_DIM].astype(BF16)
    v = zv_ref[...]
    vb_ref[...] = v.astype(BF16)
    if feature_major:
        k_ref[0] = k.T
        ki_ref[0] = ki.T[:IDX_DIM]
        v_ref[0] = v.T
    else:
        k_ref[...] = k
        ki_ref[...] = ki[:, :IDX_DIM]
        v_ref[...] = v


def _qkrope(z, cos_t, sin_t, qg, kg, tm, table_tiles, feature_major):
    m = z.shape[0]
    zc = lambda name, w: pl.BlockSpec((tm, w), lambda i, c=_col(name, w): (i, c))
    tab = pl.BlockSpec((tm, LANES), lambda i: (i % table_tiles, 0))
    vec = pl.BlockSpec((1, LANES), lambda i: (0, 0))
    row = lambda w: pl.BlockSpec((tm, w), lambda i: (i, 0))
    qi_w = IDX_HEADS * IDX_DIM
    if feature_major:
        nseq, t = m // (tm * table_tiles), tm * table_tiles
        fm = lambda w: pl.BlockSpec((1, w, tm), lambda i: (i // table_tiles, 0, i % table_tiles))
        f32_spec = lambda w: fm(w)
        f32_shape = lambda w: jax.ShapeDtypeStruct((nseq, w, t), F32)
    else:
        f32_spec = row
        f32_shape = lambda w: jax.ShapeDtypeStruct((m, w), F32)
    return pl.pallas_call(
        functools.partial(_qkrope_kernel, feature_major=feature_major),
        grid=(m // tm,),
        in_specs=[zc("zq", D_B), zc("zk", KV_DIM), zc("zqi", qi_w), zc("zki", LANES), zc("zv", KV_DIM),
                  tab, tab, vec, vec],
        out_specs=[row(D_B), f32_spec(KV_DIM), row(qi_w), f32_spec(IDX_DIM), f32_spec(KV_DIM),
                   row(KV_DIM), row(KV_DIM), row(IDX_DIM)],
        out_shape=[jax.ShapeDtypeStruct((m, D_B), F32), f32_shape(KV_DIM),
                   jax.ShapeDtypeStruct((m, qi_w), F32), f32_shape(IDX_DIM), f32_shape(KV_DIM),
                   jax.ShapeDtypeStruct((m, KV_DIM), BF16), jax.ShapeDtypeStruct((m, KV_DIM), BF16),
                   jax.ShapeDtypeStruct((m, IDX_DIM), BF16)],
        compiler_params=_params(("parallel",)),
        name="qkrope",
    )(z, z, z, z, z, cos_t, sin_t, qg, kg)


def _rope_tables(pos):
    half = HEAD_DIM // 2
    inv = ROPE_THETA ** (-jnp.arange(half, dtype=F32) / half)
    ang = pos.astype(F32)[:, None] * inv[None, :]
    cos = jnp.cos(ang)
    sin = jnp.sin(ang)
    cos64 = jnp.concatenate([cos, cos], axis=1)
    sin64 = jnp.concatenate([-sin, sin], axis=1)
    reps = LANES // HEAD_DIM
    return jnp.tile(cos64, (1, reps)), jnp.tile(sin64, (1, reps))


_SIGN = np.int32(-2 ** 31)
_MAG = np.int32(0x7FFFFFFF)
QK_SCALE = HEAD_DIM ** -0.5 * math.log2(math.e)
SEARCH_UNROLL = 4
SAMPLE_DIGIT_BITS = 3


def _dot_nt(a, b):
    return lax.dot_general(a, b, (((1,), (1,)), ((), ())), preferred_element_type=F32)


def _row_count(mask):
    return jnp.sum(mask.astype(F32), axis=1, keepdims=True)


def _ordered_to_float(t):
    key = t ^ _SIGN
    return lax.bitcast_convert_type(jnp.where(key < 0, key ^ _MAG, key), F32)


def _threshold_bits(sc_ref, kf):
    nq, nl = sc_ref.shape

    def cond(c):
        i, _, cnt_t = c
        return (i < 32) & (jnp.max(jnp.abs(cnt_t - kf)) > 0.0)

    def body(c):
        i, t, cnt_t = c
        for j in range(SEARCH_UNROLL):
            cand = t | jnp.left_shift(jnp.int32(1), 31 - (i + j))
            cnt = _row_count(sc_ref[...] >= _ordered_to_float(cand))
            ok = cnt >= kf
            t = jnp.where(ok, cand, t)
            cnt_t = jnp.where(ok, cnt, cnt_t)
        return i + SEARCH_UNROLL, t, cnt_t

    init = (jnp.int32(0), jnp.zeros((nq, 1), I32), jnp.full((nq, 1), float(nl), F32))
    return lax.while_loop(cond, body, init)[1:]


def _threshold_digits(sc_ref, kf, digit_bits):
    nq, nl = sc_ref.shape
    t = jnp.zeros((nq, 1), I32)
    cnt_t = jnp.full((nq, 1), float(nl), F32)
    pos = 32
    while pos > 0:
        nb = (pos % digit_bits) or digit_bits
        pos -= nb
        digit = jnp.zeros((nq, 1), I32)
        for j in range(1, 2 ** nb):
            cand = t | np.uint32(j << pos).astype(np.int32)
            cnt = _row_count(sc_ref[...] >= _ordered_to_float(cand))
            ok = cnt >= kf
            digit = digit + ok.astype(I32)
            cnt_t = jnp.where(ok, cnt, cnt_t)
        t = t | jnp.left_shift(digit, pos)
    return t, cnt_t


def _select_topk(sc_ref, sel_ref, topk, digit_bits=1):
    nq, nl = sc_ref.shape
    if nl == topk:
        sel_ref[...] = jnp.ones((nq, nl), F32)
        return
    kf = float(topk)
    t, cnt_t = (_threshold_bits(sc_ref, kf) if digit_bits == 1
                else _threshold_digits(sc_ref, kf, digit_bits))
    tau = jnp.where(t == 0, -jnp.inf, _ordered_to_float(t))
    sel_ref[...] = (sc_ref[...] >= tau).astype(F32)
    tie_rows = (cnt_t > kf) & (tau > NEG)

    @pl.when(jnp.max(tie_rows.astype(F32)) > 0.0)
    def _():
        need = kf - _row_count(sc_ref[...] > tau)
        tri = (lax.broadcasted_iota(I32, (LANES, LANES), 0)
               < lax.broadcasted_iota(I32, (LANES, LANES), 1)).astype(BF16)
        run = jnp.zeros((nq, 1), F32)
        for j in range(nl // LANES):
            sl = slice(j * LANES, (j + 1) * LANES)
            sj = sc_ref[:, sl]
            eqj = sj == tau
            rank = jnp.dot(eqj.astype(BF16), tri, preferred_element_type=F32) + run
            sel_ref[:, sl] = ((sj > tau) | (eqj & (rank < need))).astype(F32)
            run = run + _row_count(eqj)


def _dsa_scores(qi, wi, causal, kib_ref):
    kib = kib_ref[...]
    qis = (qi * (IDX_DIM ** -0.5)).astype(BF16)
    wis = wi * (IDX_HEADS ** -0.5)
    scores = jnp.zeros(causal.shape, F32)
    for h in range(IDX_HEADS):
        s = _dot_nt(qis[:, h * IDX_DIM:(h + 1) * IDX_DIM], kib)
        scores = scores + jnp.maximum(s, 0.0) * wis[:, h:h + 1]
    return jnp.where(causal, scores, NEG)


def _dsa_attend(q, bgate, sel, kb_ref, vb_ref):
    nq = q.shape[0]
    lane = _lane_iota((nq, LANES))
    group_w = N_HEADS // N_KV_HEADS
    kb = kb_ref[...]
    vb = vb_ref[...]
    v_lane = _lane_iota(vb.shape) // HEAD_DIM
    vones = [jnp.where(v_lane == g, vb, jnp.ones_like(vb)) for g in range(N_KV_HEADS)]
    qs = q * QK_SCALE
    outs = []
    for h in range(N_HEADS):
        g = h // group_w
        c = (h * HEAD_DIM) // LANES
        x = qs[:, c * LANES:(c + 1) * LANES]
        if (h % 2) != g:
            x = pltpu.roll(x, HEAD_DIM, 1)
        in_g = (lane // HEAD_DIM) == g
        xq = jnp.where(in_g, x, 0.0).astype(BF16)
        s = jnp.where(sel, _dot_nt(xq, kb), NEG)
        mx = jnp.max(s, axis=1, keepdims=True)
        p = jnp.exp2(s - mx)
        o = jnp.dot(p.astype(BF16), vones[g], preferred_element_type=F32)
        den_lane = ((g + 1) % N_KV_HEADS) * HEAD_DIM
        o = o / o[:, den_lane:den_lane + 1]
        o = jnp.where(in_g, o, 0.0)
        if (h % 2) != g:
            o = pltpu.roll(o, HEAD_DIM, 1)
        outs.append(o)
    cols = [outs[2 * c] + outs[2 * c + 1] for c in range(N_HEADS // 2)]
    return jnp.concatenate(cols, axis=1) * _silu(bgate)


def _dsa_prompt_kernel(acc_hbm, q_ref, qi_ref, wi_ref, bg_ref, kb_ref, vb_ref, kib_ref, o_ref, sc_ref, sel_ref,
                       *, topk, q_lo):
    del acc_hbm
    qb = pl.program_id(1) + q_lo
    nseq, nq, _ = q_ref.shape
    nl = kb_ref.shape[1]
    qpos = qb * nq + lax.broadcasted_iota(I32, (nq, 1), 0)
    causal = _lane_iota((nq, nl)) <= qpos
    for r in range(nseq):
        sc_ref[r * nq:(r + 1) * nq, :] = _dsa_scores(qi_ref[r], wi_ref[r], causal, kib_ref.at[r])
    _select_topk(sc_ref, sel_ref, topk)
    for r in range(nseq):
        sel = (sel_ref[r * nq:(r + 1) * nq, :] > 0.0) & causal
        o_ref[r] = _dsa_attend(q_ref[r], bg_ref[r], sel, kb_ref.at[r], vb_ref.at[r])


def _dsa_prompt_bucket(acc, q3, qi3, z3, kb3, vb3, kib3, nq, q_lo, q_hi):
    b, t, _ = q3.shape
    topk = min(TOPK_MAX, t // 4)
    nl = q_hi * nq
    qi_w = IDX_HEADS * IDX_DIM
    ns = math.gcd(b, DSA_SEQS)
    qblk = lambda w: pl.BlockSpec((ns, nq, w), lambda bi, i: (bi, i + q_lo, 0))
    zblk = lambda name, w: pl.BlockSpec((ns, nq, w), lambda bi, i, c=_col(name, w): (bi, i + q_lo, c))
    keys = lambda w: pl.BlockSpec((ns, nl, w), lambda bi, i: (bi, 0, 0))
    return pl.pallas_call(
        functools.partial(_dsa_prompt_kernel, topk=topk, q_lo=q_lo),
        grid=(b // ns, q_hi - q_lo),
        in_specs=[pl.BlockSpec(memory_space=pl.ANY),
                  qblk(D_B), qblk(qi_w), zblk("zwi", LANES), zblk("b_gate", D_B),
                  keys(KV_DIM), keys(KV_DIM), keys(IDX_DIM)],
        out_specs=pl.BlockSpec((ns, nq, D_B), lambda bi, i: (bi, i + q_lo, 0)),
        out_shape=jax.ShapeDtypeStruct((b, t, D_B), F32),
        input_output_aliases={0: 0},
        scratch_shapes=[pltpu.VMEM((ns * nq, nl), F32), pltpu.VMEM((ns * nq, nl), F32)],
        compiler_params=_params(("parallel", "parallel")),
        name="dsa_prompt",
    )(acc, q3, qi3, z3, z3, kb3, vb3, kib3)


DSA_BUCKETS = 8
DSA_ROWS = 256
DSA_SEQS = 4


def _dsa_prompt(q3, qi3, z3, kb3, vb3, kib3):
    t = q3.shape[1]
    nq = DSA_ROWS if t % DSA_ROWS == 0 else Q_BLOCK
    nblk = t // nq
    step = max(1, nblk // DSA_BUCKETS)
    acc = jnp.zeros(q3.shape[:2] + (D_B,), F32)
    for lo in range(0, nblk, step):
        acc = _dsa_prompt_bucket(acc, q3, qi3, z3, kb3, vb3, kib3, nq, lo, min(lo + step, nblk))
    return acc


def _dsa_sample_kernel(pt_ref, q_ref, qi_ref, wi_ref, bg_ref, knt_ref, vnt_ref, kint_ref,
                       ck_hbm, cv_hbm, cki_hbm, o_ref, kt_ref, vt_ref, kit_ref, sc_ref, sel_ref, sems,
                       *, topk, n_pages, page_base, ts):
    bi = pl.program_id(0)
    nb = pl.num_programs(0)
    nq = q_ref.shape[1]
    past = n_pages * PAGE_SIZE
    nl = kt_ref.shape[2]
    group_w = N_HEADS // N_KV_HEADS
    slot = bi % 2

    def copies(seq, s, p):
        page = pt_ref[seq, p] + page_base
        dst = pl.ds(p * PAGE_SIZE, PAGE_SIZE)
        return (pltpu.make_async_copy(ck_hbm.at[page], kt_ref.at[s, :, dst], sems.at[s, 0]),
                pltpu.make_async_copy(cv_hbm.at[page], vt_ref.at[s, :, dst], sems.at[s, 1]),
                pltpu.make_async_copy(cki_hbm.at[page], kit_ref.at[s, :, dst], sems.at[s, 2]))

    def start_gather(seq, s):
        for p in range(n_pages):
            for cp in copies(seq, s, p):
                cp.start()

    @pl.when(bi == 0)
    def _():
        start_gather(0, 0)

    @pl.when(bi + 1 < nb)
    def _():
        start_gather(bi + 1, 1 - slot)

    kt_ref[slot, :, past:nl] = knt_ref[0]
    vt_ref[slot, :, past:nl] = vnt_ref[0]
    kit_ref[slot, :, past:nl] = kint_ref[0]

    for p in range(n_pages):
        for cp in copies(bi, slot, p):
            cp.wait()
    kt_ref, vt_ref, kit_ref = kt_ref.at[slot], vt_ref.at[slot], kit_ref.at[slot]

    row = lax.broadcasted_iota(I32, (nq, 1), 0)
    qpos = past + jnp.minimum(row, ts - 1)
    stack = lambda x, w, heads: jnp.concatenate([x[:, h * w:(h + 1) * w] for h in heads], axis=0)

    qis = stack(qi_ref[0] * (IDX_DIM ** -0.5), IDX_DIM, range(IDX_HEADS)).astype(BF16)
    s = jnp.dot(qis, kit_ref[...].astype(BF16), preferred_element_type=F32)
    wis = wi_ref[0] * (IDX_HEADS ** -0.5)
    scores = jnp.zeros((nq, nl), F32)
    for h in range(IDX_HEADS):
        scores = scores + jnp.maximum(s[h * nq:(h + 1) * nq], 0.0) * wis[:, h:h + 1]
    causal = _lane_iota((nq, nl)) <= qpos
    sc_ref[...] = jnp.where(causal, scores, NEG)
    _select_topk(sc_ref, sel_ref, topk, digit_bits=SAMPLE_DIGIT_BITS)

    sel = (sel_ref[...] > 0.0) & causal
    sel_g = jnp.concatenate([sel] * group_w, axis=0)
    qs = q_ref[0] * QK_SCALE
    outs = []
    for g in range(N_KV_HEADS):
        rows = slice(g * HEAD_DIM, (g + 1) * HEAD_DIM)
        qg = stack(qs, HEAD_DIM, range(g * group_w, (g + 1) * group_w)).astype(BF16)
        sg = jnp.dot(qg, kt_ref[rows, :].astype(BF16), preferred_element_type=F32)
        sg = jnp.where(sel_g, sg, NEG)
        mx = jnp.max(sg, axis=1, keepdims=True)
        p = jnp.exp2(sg - mx)
        den = jnp.sum(p, axis=1, keepdims=True)
        og = _dot_nt(p.astype(BF16), vt_ref[rows, :].astype(BF16)) / den
        outs += [og[j * nq:(j + 1) * nq] for j in range(group_w)]
    o_ref[0] = jnp.concatenate(outs, axis=1) * _silu(bg_ref[0])


def _dsa_sample(page_table, q8, qi8, wi8, bg8, knt, vnt, kint, ckt, cvt, ckit, layer, depth, ts):
    b, nq, _ = q8.shape
    n_pages = page_table.shape[1]
    past = n_pages * PAGE_SIZE
    nl = past + LANES
    topk = min(TOPK_MAX, (past + ts) // 4)
    n_pool = ckt.shape[0] // depth
    qi_w = IDX_HEADS * IDX_DIM
    blk = lambda r, w: pl.BlockSpec((1, r, w), lambda bi, pt: (bi, 0, 0))
    anyspec = pl.BlockSpec(memory_space=pl.ANY)
    grid_spec = pltpu.PrefetchScalarGridSpec(
        num_scalar_prefetch=1,
        grid=(b,),
        in_specs=[blk(nq, D_B), blk(nq, qi_w), blk(nq, LANES), blk(nq, D_B),
                  blk(KV_DIM, LANES), blk(KV_DIM, LANES), blk(IDX_DIM, LANES),
                  anyspec, anyspec, anyspec],
        out_specs=pl.BlockSpec((1, nq, D_B), lambda bi, pt: (bi, 0, 0)),
        scratch_shapes=[pltpu.VMEM((2, KV_DIM, nl), F32), pltpu.VMEM((2, KV_DIM, nl), F32),
                        pltpu.VMEM((2, IDX_DIM, nl), F32),
                        pltpu.VMEM((nq, nl), F32), pltpu.VMEM((nq, nl), F32),
                        pltpu.SemaphoreType.DMA((2, 3))])
    return pl.pallas_call(
        functools.partial(_dsa_sample_kernel, topk=topk, n_pages=n_pages, page_base=layer * n_pool, ts=ts),
        grid_spec=grid_spec,
        out_shape=jax.ShapeDtypeStruct((b, nq, D_B), F32),
        compiler_params=_params(("arbitrary",)),
        name="dsa_sample",
    )(page_table, q8, qi8, wi8, bg8, knt, vnt, kint, ckt, cvt, ckit)


def _hgrn_kernel(cq_ref, cf_ref, ci_ref, cg_ref, lb_ref, ng_ref, s0_ref, yc_ref, s_ref, *, c, valid):
    ci_idx = pl.program_id(1)

    @pl.when(ci_idx == 0)
    def _():
        s_ref[...] = s0_ref[...]

    for r in range(cq_ref.shape[0]):
        _hgrn_chunk(r, cq_ref, cf_ref, ci_ref, cg_ref, lb_ref, ng_ref, yc_ref, s_ref, c, valid)


def _hgrn_chunk(r, cq_ref, cf_ref, ci_ref, cg_ref, lb_ref, ng_ref, yc_ref, s_ref, c, valid):
    fx = cf_ref[r]
    lb = lb_ref[...]
    log_f = jnp.log(jnp.maximum(lb, LB_FLOOR) + (1.0 - lb) * _sigmoid(fx))
    kk = (1.0 - lb) * _sigmoid(-fx)
    qc = _silu(cq_ref[r])
    iv = ci_ref[r]
    row = lax.broadcasted_iota(I32, (c, 1), 0)
    if valid < c:
        log_f = jnp.where(row < valid, log_f, 0.0)

    rr = lax.broadcasted_iota(I32, (c, c), 0)
    cc = lax.broadcasted_iota(I32, (c, c), 1)
    hs = [slice(h * C_KDIM, (h + 1) * C_KDIM) for h in range(C_HEADS)]
    att = [jnp.zeros((c, c), F32) for _ in range(C_HEADS)]
    cs = log_f
    tot = log_f
    m = 1
    while m < c:
        right = ((row // m) % 2) == 1
        qm = jnp.where(right, qc * jnp.exp(cs), 0.0).astype(BF16)
        km = jnp.where(right, 0.0, kk * jnp.exp(tot - cs)).astype(BF16)
        pair = (rr // (2 * m)) == (cc // (2 * m))
        for h in range(C_HEADS):
            att[h] = att[h] + jnp.where(pair, _dot_nt(qm[:, hs[h]], km[:, hs[h]]), 0.0)
        sib = jnp.where(right, pltpu.roll(tot, m, 0), pltpu.roll(tot, c - m, 0))
        cs = jnp.where(right, cs + sib, cs)
        tot = tot + sib
        m *= 2
    qdec = (qc * jnp.exp(cs)).astype(BF16)
    kdec = (kk * jnp.exp(tot - cs)).astype(BF16)
    ivb = iv.astype(BF16)
    eye_c = rr == cc
    eye_k = (lax.broadcasted_iota(I32, (C_KDIM, C_KDIM), 0)
             == lax.broadcasted_iota(I32, (C_KDIM, C_KDIM), 1))
    ys = []
    for h in range(C_HEADS):
        sl = hs[h]
        diag = jnp.sum(qc[:, sl] * kk[:, sl], axis=1, keepdims=True)
        a_h = att[h] + jnp.where(eye_c, diag, 0.0)
        s_h = s_ref[r, h]
        o = (jnp.dot(a_h.astype(BF16), ivb[:, sl], preferred_element_type=F32)
             + jnp.dot(qdec[:, sl], s_h.astype(BF16), preferred_element_type=F32))
        e_end = jnp.exp(tot[0:1, sl])
        e_col = jnp.sum(jnp.where(eye_k, e_end, 0.0), axis=1, keepdims=True)
        upd = lax.dot_general(kdec[:, sl], ivb[:, sl], (((0,), (0,)), ((), ())),
                              preferred_element_type=F32)
        s_ref[r, h] = e_col * s_h + upd
        ms = jnp.mean(o * o, axis=-1, keepdims=True)
        ys.append(o * lax.rsqrt(ms + EPS) * ng_ref[...])
    yc_ref[r] = jnp.concatenate(ys, axis=1) * _silu(cg_ref[r])


HGRN_SEQS = 4


def _hgrn(z3, lb, cng, s0, c, valid):
    b, t, _ = z3.shape
    nb = math.gcd(b, HGRN_SEQS)
    zblk = lambda name: pl.BlockSpec((nb, c, D_C), lambda bi, i, col=_col(name, D_C): (bi, i, col))
    sblk = pl.BlockSpec((nb, C_HEADS, C_KDIM, C_VDIM), lambda bi, i: (bi, 0, 0, 0))
    return pl.pallas_call(
        functools.partial(_hgrn_kernel, c=c, valid=valid),
        grid=(b // nb, t // c),
        in_specs=[zblk("cq"), zblk("cf"), zblk("ci"), zblk("c_gate"),
                  pl.BlockSpec((1, C_FDIM), lambda bi, i: (0, 0)),
                  pl.BlockSpec((1, C_VDIM), lambda bi, i: (0, 0)), sblk],
        out_specs=[pl.BlockSpec((nb, c, D_C), lambda bi, i: (bi, i, 0)), sblk],
        out_shape=[jax.ShapeDtypeStruct((b, t, D_C), F32),
                   jax.ShapeDtypeStruct((b, C_HEADS, C_KDIM, C_VDIM), F32)],
        compiler_params=_params(("parallel", "arbitrary")),
        name="hgrn",
    )(z3, z3, z3, z3, lb.reshape(1, C_FDIM), cng.reshape(1, C_VDIM), s0)


def _merge_kernel(x_ref, ya_ref, yb_ref, yc_ref, ga_ref, gb_ref, gc_ref, gate_ref,
                  wa_ref, wb_ref, wc_ref, wo_ref, o_ref):
    def proj(y_ref, w_ref):
        return jnp.dot(y_ref[...].astype(BF16), w_ref[...], preferred_element_type=F32)

    m = (_sigmoid(ga_ref[...]) * proj(ya_ref, wa_ref)
         + _sigmoid(gb_ref[...]) * proj(yb_ref, wb_ref)
         + _sigmoid(gc_ref[...]) * proj(yc_ref, wc_ref))
    o_ref[...] = x_ref[...] + gate_ref[0] * jnp.dot(m.astype(BF16), wo_ref[...], preferred_element_type=F32)


def _merge(x2, ya, yb, yc, z, gate3, wa, wb, wc, wo, tm, tiles_per_mod):
    m, d = x2.shape
    r = gate3.shape[1]
    row = lambda w: pl.BlockSpec((tm, w), lambda i: (i, 0))
    zc = lambda name: pl.BlockSpec((tm, d), lambda i, c=_col(name, d): (i, c))
    wspec = lambda k: pl.BlockSpec((k, d), lambda i: (0, 0))
    return pl.pallas_call(
        _merge_kernel,
        grid=(m // tm,),
        in_specs=[row(d), row(D_A), row(D_B), row(D_C), zc("g_a"), zc("g_b"), zc("g_c"),
                  pl.BlockSpec((1, r, d), lambda i: (i // tiles_per_mod, 0, 0)),
                  wspec(D_A), wspec(D_B), wspec(D_C), wspec(d)],
        out_specs=row(d),
        out_shape=jax.ShapeDtypeStruct((m, d), F32),
        compiler_params=_params(("parallel",)),
        name="merge",
    )(x2, ya, yb, yc, z, z, z, gate3, wa, wb, wc, wo)


def _pick_tile(n, pref):
    t = min(pref, n)
    while n % t:
        t //= 2
    return t


def kernel(x_prompt, x_sample, cache_k, cache_v, cache_idx_k, state_conv, state_hgrn, page_table,
           c_prompt, c_sample, w_ada, b_ada, norm_g, w_in, w_dw, b_dw, ln_g, ln_b, q_norm_g, k_norm_g,
           lb_logits, c_norm_g, w_proj_a, w_proj_b, w_proj_c, w_out):
    depth = w_in.shape[0]
    bp, tp, d = x_prompt.shape
    bs, ts, _ = x_sample.shape
    mp, ms = bp * tp, bs * ts
    n_pages = page_table.shape[1]
    past = n_pages * PAGE_SIZE
    assert d == D_MODEL and tp % Q_BLOCK == 0 and tp % CHUNK == 0 and tp >= HALO
    assert ts <= SUBLANES and ms % SUBLANES == 0

    lbp = jax.nn.softmax(lb_logits.astype(F32), axis=0)
    lb_all = jnp.cumsum(lbp, axis=0) - lbp[0:1]
    cos_p, sin_p = _rope_tables(jnp.arange(tp))
    cos_s, sin_s = _rope_tables(past + (jnp.arange(ms) % ts))
    ckt = jnp.transpose(cache_k, (0, 1, 3, 4, 2)).reshape(-1, KV_DIM, PAGE_SIZE)
    cvt = jnp.transpose(cache_v, (0, 1, 3, 4, 2)).reshape(-1, KV_DIM, PAGE_SIZE)
    ckit = jnp.transpose(cache_idx_k, (0, 1, 3, 2)).reshape(-1, IDX_DIM, PAGE_SIZE)
    c_all = jnp.concatenate([c_prompt, c_sample], axis=0)

    tm_p = _pick_tile(mp, 1024)
    tm_p = _pick_tile(tp, tm_p)
    tm_in = _pick_tile(tp, 2048)
    tt = _pick_tile(tp, 512)
    tm_g = _pick_tile(tp, 512)
    zeros_buf = jnp.zeros((bp, HALO, D_A), F32)
    s00 = jnp.zeros((bp, C_HEADS, C_KDIM, C_VDIM), F32)
    pad_rows = lambda a3: jnp.pad(a3, ((0, 0), (0, SUBLANES - ts), (0, 0)))

    xp = x_prompt.reshape(mp, d)
    xs = x_sample.reshape(ms, d)
    outs = [[] for _ in range(10)]
    for l in range(depth):
        mod = _ada(c_all, w_ada[l].astype(BF16), b_ada[l])
        shift, scale, gate = mod[:, :d], mod[:, d:2 * d], mod[:, 2 * d:]
        w_in_t = _pack_w_in_t(w_in[l]).astype(BF16)
        wa, wb, wc, wo = (w_proj_a[l].astype(BF16), w_proj_b[l].astype(BF16),
                          w_proj_c[l].astype(BF16), w_out[l].astype(BF16))
        qg = jnp.tile(q_norm_g[l], LANES // HEAD_DIM).reshape(1, LANES)
        kg = jnp.tile(k_norm_g[l], LANES // HEAD_DIM).reshape(1, LANES)
        lb = lb_all[l]

        per_seq = lambda a: a[:bp].reshape(bp, 1, d)
        h = _prenorm(xp, norm_g[l], per_seq(scale), per_seq(shift), tm_p, tp // tm_p)
        z = _matmul_nt(h, w_in_t, tm_in, NZ // 6)
        z3 = z.reshape(bp, tp, NZ)
        ya, utail = _conv_prompt(z3, zeros_buf, w_dw[l], b_dw[l], ln_g[l], ln_b[l], tt)
        q, kt, qi, kit, vt, kb, vb, kib = _qkrope(z, cos_p, sin_p, qg, kg, tm_p, tp // tm_p, True)
        r3 = lambda a: a.reshape(bp, tp, a.shape[-1])
        yb = _dsa_prompt(r3(q), r3(qi), z3, r3(kb), r3(vb), r3(kib))
        yc, s_p = _hgrn(z3, lb, c_norm_g[l], s00, CHUNK, CHUNK)
        xp = _merge(xp, ya.reshape(mp, D_A), yb.reshape(mp, D_B), yc.reshape(mp, D_C), z,
                    per_seq(gate), wa, wb, wc, wo, tm_g, tp // tm_g)
        heads_last = lambda a: jnp.transpose(a.reshape(bp, N_KV_HEADS, HEAD_DIM, tp), (0, 3, 1, 2))
        outs[0].append(heads_last(kt))
        outs[1].append(heads_last(vt))
        outs[2].append(jnp.swapaxes(kit, 1, 2))
        outs[3].append(utail[:, HALO - (CONV_W - 1):, :])
        outs[4].append(s_p)

        per_row = lambda a: jnp.repeat(a[bp:], ts, axis=0).reshape(1, ms, d)
        hs = _prenorm(xs, norm_g[l], per_row(scale), per_row(shift), ms, 1)
        zs = _matmul_nt(hs, w_in_t, ms, NZ // 6)
        zs3 = zs.reshape(bs, ts, NZ)
        seg_t = lambda name: jnp.swapaxes(
            zs3[:, :, _PACK_OFF[name]:_PACK_OFF[name] + D_A], 0, 1)
        ya_t, u_t = _conv_sample(seg_t("a_val"), seg_t("a_glu"), seg_t("a_gate"),
                                 jnp.swapaxes(state_conv[l], 0, 1), w_dw[l], b_dw[l], ln_g[l], ln_b[l])
        ya_s = jnp.swapaxes(ya_t, 0, 1).reshape(ms, D_A)
        q, k, qi, ki, v, _, _, _ = _qkrope(zs, cos_s, sin_s, qg, kg, ms, 1, False)
        p8 = lambda a: pad_rows(a.reshape(bs, ts, a.shape[-1]))
        zs8 = pad_rows(zs3)
        seg8 = lambda name, w: zs8[:, :, _PACK_OFF[name]:_PACK_OFF[name] + w]
        new_t = lambda a: jnp.pad(jnp.swapaxes(a.reshape(bs, ts, a.shape[-1]), 1, 2),
                                  ((0, 0), (0, 0), (0, LANES - ts)))
        yb8 = _dsa_sample(page_table, p8(q), p8(qi), seg8("zwi", LANES), seg8("b_gate", D_B),
                          new_t(k), new_t(v), new_t(ki), ckt, cvt, ckit, l, depth, ts)
        yc8, s_s = _hgrn(zs8, lb, c_norm_g[l], state_hgrn[l], SUBLANES, ts)
        xs = _merge(xs, ya_s, yb8[:, :ts].reshape(ms, D_B), yc8[:, :ts].reshape(ms, D_C), zs,
                    per_row(gate), wa, wb, wc, wo, ms, 1)
        outs[5].append(k.reshape(bs, ts, N_KV_HEADS, HEAD_DIM))
        outs[6].append(v.reshape(bs, ts, N_KV_HEADS, HEAD_DIM))
        outs[7].append(ki.reshape(bs, ts, IDX_DIM))
        outs[8].append(jnp.concatenate([state_conv[l][:, ts:], jnp.swapaxes(u_t, 0, 1)], axis=1))
        outs[9].append(s_s)

    st = [jnp.stack(o) for o in outs]
    return (xp.reshape(bp, tp, d), xs.reshape(bs, ts, d), st[0], st[1], st[2], st[3], st[4],
            st[5], st[6], st[7], st[8], st[9])
```

```python
import functools
import math

import jax
import jax.numpy as jnp
import numpy as np
from jax import lax
from jax.experimental import pallas as pl
from jax.experimental.pallas import tpu as pltpu

F32 = jnp.float32
BF16 = jnp.bfloat16
I32 = jnp.int32

LANES = 128
SUBLANES = 8
VMEM_LIMIT = 56 * 1024 * 1024

D_MODEL = 1024
EPS = 1e-6
NEG = -1e30
LB_FLOOR = 1e-30
D_A = D_MODEL // 2
CONV_W = 31
HALO = 32
N_HEADS = 8
N_KV_HEADS = 2
HEAD_DIM = 64
D_B = N_HEADS * HEAD_DIM
KV_DIM = N_KV_HEADS * HEAD_DIM
IDX_HEADS = 4
IDX_DIM = 64
TOPK_MAX = 256
ROPE_THETA = 10000.0
Q_BLOCK = 128
C_HEADS = 4
C_KDIM = 128
C_VDIM = D_MODEL // 2 // C_HEADS
D_C = C_HEADS * C_VDIM
C_FDIM = C_HEADS * C_KDIM
CHUNK = 64
PAGE_SIZE = 128

_IN_NAMES = ("a_val", "a_glu", "a_gate", "zq", "zk", "zv", "zqi", "zki", "zwi", "b_gate",
             "cq", "cf", "ci", "c_gate", "g_a", "g_b", "g_c")
_IN_WIDTHS = (D_A, D_A, D_A, D_B, KV_DIM, KV_DIM, IDX_HEADS * IDX_DIM, IDX_DIM, IDX_HEADS, D_B,
              C_FDIM, C_FDIM, D_C, D_C, D_MODEL, D_MODEL, D_MODEL)
_IN_OFFS = dict(zip(_IN_NAMES, np.concatenate([[0], np.cumsum(_IN_WIDTHS)[:-1]]).tolist()))
_IN_W = dict(zip(_IN_NAMES, _IN_WIDTHS))
_PACK_ORDER = ("g_a", "g_b", "g_c", "a_val", "a_glu", "a_gate", "zq", "b_gate", "cq", "cf", "ci",
               "c_gate", "zqi", "zk", "zv", "zki", "zwi")


def _round_up(n, m):
    return (n + m - 1) // m * m


_PACK_W = {n: _round_up(_IN_W[n], LANES) for n in _PACK_ORDER}
_PACK_OFF = {}
_o = 0
for _n in _PACK_ORDER:
    assert _o % _PACK_W[_n] == 0
    _PACK_OFF[_n] = _o
    _o += _PACK_W[_n]
NZ = _o


def _pack_w_in_t(w):
    wt = jnp.swapaxes(w, 0, 1)
    parts = []
    for n in _PACK_ORDER:
        seg = wt[_IN_OFFS[n]:_IN_OFFS[n] + _IN_W[n]]
        pad = _PACK_W[n] - _IN_W[n]
        if pad:
            seg = jnp.pad(seg, ((0, pad), (0, 0)))
        parts.append(seg)
    return jnp.concatenate(parts, axis=0)


def _col(name, width):
    assert _PACK_OFF[name] % width == 0
    return _PACK_OFF[name] // width


def _sigmoid(x):
    return jax.nn.sigmoid(x)


def _silu(x):
    return x * jax.nn.sigmoid(x)


def _params(sem):
    return pltpu.CompilerParams(dimension_semantics=sem, vmem_limit_bytes=VMEM_LIMIT)


def _ada_kernel(c_ref, w_ref, b_ref, o_ref):
    c = c_ref[...]
    o_ref[...] = jnp.dot(_silu(c).astype(BF16), w_ref[...], preferred_element_type=F32) + b_ref[...]


def _ada(c, w_bf, b):
    n, d = c.shape
    nout = w_bf.shape[1]
    tn = D_MODEL
    return pl.pallas_call(
        _ada_kernel,
        grid=(nout // tn,),
        in_specs=[pl.BlockSpec((n, d), lambda j: (0, 0)),
                  pl.BlockSpec((d, tn), lambda j: (0, j)),
                  pl.BlockSpec((1, tn), lambda j: (0, j))],
        out_specs=pl.BlockSpec((n, tn), lambda j: (0, j)),
        out_shape=jax.ShapeDtypeStruct((n, nout), F32),
        compiler_params=_params(("arbitrary",)),
        name="ada",
    )(c, w_bf, b.reshape(1, nout))


def _prenorm_kernel(x_ref, g_ref, sc_ref, sh_ref, o_ref):
    x = x_ref[...]
    ms = jnp.mean(x * x, axis=-1, keepdims=True)
    y = x * lax.rsqrt(ms + EPS) * g_ref[...]
    o_ref[...] = (y * (1.0 + sc_ref[0]) + sh_ref[0]).astype(o_ref.dtype)


def _prenorm(x2, g, scale3, shift3, tm, tiles_per_mod):
    m, d = x2.shape
    r = scale3.shape[1]
    return pl.pallas_call(
        _prenorm_kernel,
        grid=(m // tm,),
        in_specs=[pl.BlockSpec((tm, d), lambda i: (i, 0)),
                  pl.BlockSpec((1, d), lambda i: (0, 0)),
                  pl.BlockSpec((1, r, d), lambda i: (i // tiles_per_mod, 0, 0)),
                  pl.BlockSpec((1, r, d), lambda i: (i // tiles_per_mod, 0, 0))],
        out_specs=pl.BlockSpec((tm, d), lambda i: (i, 0)),
        out_shape=jax.ShapeDtypeStruct((m, d), BF16),
        compiler_params=_params(("parallel",)),
        name="prenorm",
    )(x2, g.reshape(1, d), scale3, shift3)


def _mm_kernel(a_ref, bt_ref, o_ref):
    o_ref[...] = lax.dot_general(a_ref[...], bt_ref[...], (((1,), (1,)), ((), ())),
                                 preferred_element_type=F32)


def _matmul_nt(a_bf, bt_bf, tm, tn):
    m, k = a_bf.shape
    n = bt_bf.shape[0]
    return pl.pallas_call(
        _mm_kernel,
        grid=(n // tn, m // tm),
        in_specs=[pl.BlockSpec((tm, k), lambda j, i: (i, 0)),
                  pl.BlockSpec((tn, k), lambda j, i: (j, 0))],
        out_specs=pl.BlockSpec((tm, tn), lambda j, i: (i, j)),
        out_shape=jax.ShapeDtypeStruct((m, n), F32),
        compiler_params=_params(("parallel", "parallel")),
        name="inproj",
    )(a_bf, bt_bf)


def _ln_swish_gate(y, gate, bdw, lng, lnb):
    y = y + bdw
    mu = jnp.mean(y, axis=-1, keepdims=True)
    yc = y - mu
    var = jnp.mean(yc * yc, axis=-1, keepdims=True)
    yn = yc * lax.rsqrt(var + EPS) * lng + lnb
    return _silu(yn) * _silu(gate)


def _conv_prompt_kernel(val_ref, glu_ref, gate_ref, valh_ref, gluh_ref, buf_ref, w_ref, bdw_ref,
                        lng_ref, lnb_ref, ya_ref, utail_ref, f_ref, g_ref, *, tt):
    ti = pl.program_id(1)
    u = val_ref[0] * _sigmoid(glu_ref[0])
    uh = valh_ref[0] * _sigmoid(gluh_ref[0])
    f_ref[0:HALO, :] = jnp.where(ti == 0, buf_ref[0], uh)
    f_ref[HALO:HALO + tt, :] = u
    nrows = g_ref.shape[1]
    for p in range(1, SUBLANES):
        g_ref[p - 1] = f_ref[pl.ds(p, nrows), :]
    off = HALO - (CONV_W - 1)
    acc = jnp.zeros((tt, D_A), F32)
    for j in range(CONV_W):
        p = (off + j) % SUBLANES
        a = off + j - p
        win = f_ref[pl.ds(a, tt), :] if p == 0 else g_ref[p - 1, pl.ds(a, tt), :]
        acc = acc + w_ref[j:j + 1, :] * win
    ya_ref[0] = _ln_swish_gate(acc, gate_ref[0], bdw_ref[...], lng_ref[...], lnb_ref[...])

    @pl.when(ti == pl.num_programs(1) - 1)
    def _():
        utail_ref[0] = f_ref[tt:tt + HALO, :]


def _conv_prompt(z3, buf32, w_dw, b_dw, ln_g, ln_b, tt):
    b, t, _ = z3.shape
    hb = tt // HALO
    cur = lambda name: pl.BlockSpec((1, tt, D_A), lambda bi, ti, c=_col(name, D_A): (bi, ti, c))
    halo = lambda name: pl.BlockSpec(
        (1, HALO, D_A), lambda bi, ti, c=_col(name, D_A): (bi, jnp.maximum(ti * hb - 1, 0), c))
    vec = pl.BlockSpec((1, D_A), lambda bi, ti: (0, 0))
    wpad = jnp.pad(w_dw, ((0, HALO - CONV_W), (0, 0)))
    return pl.pallas_call(
        functools.partial(_conv_prompt_kernel, tt=tt),
        grid=(b, t // tt),
        in_specs=[cur("a_val"), cur("a_glu"), cur("a_gate"), halo("a_val"), halo("a_glu"),
                  pl.BlockSpec((1, HALO, D_A), lambda bi, ti: (bi, 0, 0)),
                  pl.BlockSpec((HALO, D_A), lambda bi, ti: (0, 0)), vec, vec, vec],
        out_specs=[pl.BlockSpec((1, tt, D_A), lambda bi, ti: (bi, ti, 0)),
                   pl.BlockSpec((1, HALO, D_A), lambda bi, ti: (bi, 0, 0))],
        out_shape=[jax.ShapeDtypeStruct((b, t, D_A), F32),
                   jax.ShapeDtypeStruct((b, HALO, D_A), F32)],
        scratch_shapes=[pltpu.VMEM((HALO + tt, D_A), F32),
                        pltpu.VMEM((SUBLANES - 1, HALO + tt - SUBLANES, D_A), F32)],
        compiler_params=_params(("parallel", "arbitrary")),
        name="conv_prompt",
    )(z3, z3, z3, z3, z3, buf32, wpad, b_dw.reshape(1, D_A), ln_g.reshape(1, D_A), ln_b.reshape(1, D_A))


def _conv_sample_kernel(val_ref, glu_ref, gate_ref, buf_ref, w_ref, bdw_ref, lng_ref, lnb_ref,
                        ya_ref, u_ref, *, ts):
    nb = CONV_W - 1
    for t in range(ts):
        u_ref[t] = val_ref[t] * _sigmoid(glu_ref[t])
    for t in range(ts):
        acc = jnp.zeros(u_ref.shape[1:], F32)
        for j in range(CONV_W):
            r = t + j
            src = buf_ref[r] if r < nb else u_ref[r - nb]
            acc = acc + w_ref[j:j + 1, :] * src
        ya_ref[t] = _ln_swish_gate(acc, gate_ref[t], bdw_ref[...], lng_ref[...], lnb_ref[...])


def _conv_sample(val_t, glu_t, gate_t, buf_t, w_dw, b_dw, ln_g, ln_b):
    ts, b, _ = val_t.shape
    assert ts <= CONV_W - 1
    return pl.pallas_call(
        functools.partial(_conv_sample_kernel, ts=ts),
        out_shape=[jax.ShapeDtypeStruct((ts, b, D_A), F32), jax.ShapeDtypeStruct((ts, b, D_A), F32)],
        compiler_params=pltpu.CompilerParams(vmem_limit_bytes=VMEM_LIMIT),
        name="conv_sample",
    )(val_t, glu_t, gate_t, buf_t, w_dw, b_dw.reshape(1, D_A), ln_g.reshape(1, D_A), ln_b.reshape(1, D_A))


def _lane_iota(shape):
    return lax.broadcasted_iota(I32, shape, len(shape) - 1)


def _swap_half(x):
    w = x.shape[-1]
    half = HEAD_DIM // 2
    first = (_lane_iota(x.shape) % HEAD_DIM) < half
    return jnp.where(first, pltpu.roll(x, w - half, 1), pltpu.roll(x, half, 1))


def _group_sum(s):
    same_group = (lax.broadcasted_iota(I32, (LANES, LANES), 0) // HEAD_DIM
                  == lax.broadcasted_iota(I32, (LANES, LANES), 1) // HEAD_DIM).astype(BF16)
    hi = s.astype(BF16)
    r1 = s - hi.astype(F32)
    mid = r1.astype(BF16)
    lo = (r1 - mid.astype(F32)).astype(BF16)
    cols = []
    for c in range(s.shape[-1] // LANES):
        sl = slice(c * LANES, (c + 1) * LANES)
        cols.append(jnp.dot(hi[:, sl], same_group, preferred_element_type=F32)
                    + jnp.dot(mid[:, sl], same_group, preferred_element_type=F32)
                    + jnp.dot(lo[:, sl], same_group, preferred_element_type=F32))
    return cols[0] if len(cols) == 1 else jnp.concatenate(cols, axis=-1)


def _tile_lanes(t, w):
    reps = w // t.shape[-1]
    return t if reps == 1 else jnp.concatenate([t] * reps, axis=-1)


def _rope(x, cos, sin):
    w = x.shape[-1]
    return x * _tile_lanes(cos, w) + _swap_half(x) * _tile_lanes(sin, w)


def _head_rms(x, g):
    ms = _group_sum(x * x) * (1.0 / HEAD_DIM)
    y = x * lax.rsqrt(ms + EPS)
    return y if g is None else y * _tile_lanes(g, x.shape[-1])


def _qkrope_kernel(zq_ref, zk_ref, zqi_ref, zki_ref, zv_ref, cos_ref, sin_ref, qg_ref, kg_ref,
                   q_ref, k_ref, qi_ref, ki_ref, v_ref, kb_ref, vb_ref, kib_ref, *, feature_major):
    cos = cos_ref[...]
    sin = sin_ref[...]
    q_ref[...] = _rope(_head_rms(zq_ref[...], qg_ref[...]), cos, sin)
    k = _rope(_head_rms(zk_ref[...], kg_ref[...]), cos, sin)
    kb_ref[...] = k.astype(BF16)
    qi_ref[...] = _rope(zqi_ref[...], cos, sin)
    ki = _rope(_head_rms(zki_ref[...], None), cos, sin)
    kib_ref[...] = ki[:, :IDX_DIM].astype(BF16)
    v = zv_ref[...]
    vb_ref[...] = v.astype(BF16)
    if feature_major:
        k_ref[0] = k.T
        ki_ref[0] = ki.T[:IDX_DIM]
        v_ref[0] = v.T
    else:
        k_ref[...] = k
        ki_ref[...] = ki[:, :IDX_DIM]
        v_ref[...] = v


def _qkrope(z, cos_t, sin_t, qg, kg, tm, table_tiles, feature_major):
    m = z.shape[0]
    zc = lambda name, w: pl.BlockSpec((tm, w), lambda i, c=_col(name, w): (i, c))
    tab = pl.BlockSpec((tm, LANES), lambda i: (i % table_tiles, 0))
    vec = pl.BlockSpec((1, LANES), lambda i: (0, 0))
    row = lambda w: pl.BlockSpec((tm, w), lambda i: (i, 0))
    qi_w = IDX_HEADS * IDX_DIM
    if feature_major:
        nseq, t = m // (tm * table_tiles), tm * table_tiles
        fm = lambda w: pl.BlockSpec((1, w, tm), lambda i: (i // table_tiles, 0, i % table_tiles))
        f32_spec = lambda w: fm(w)
        f32_shape = lambda w: jax.ShapeDtypeStruct((nseq, w, t), F32)
    else:
        f32_spec = row
        f32_shape = lambda w: jax.ShapeDtypeStruct((m, w), F32)
    return pl.pallas_call(
        functools.partial(_qkrope_kernel, feature_major=feature_major),
        grid=(m // tm,),
        in_specs=[zc("zq", D_B), zc("zk", KV_DIM), zc("zqi", qi_w), zc("zki", LANES), zc("zv", KV_DIM),
                  tab, tab, vec, vec],
        out_specs=[row(D_B), f32_spec(KV_DIM), row(qi_w), f32_spec(IDX_DIM), f32_spec(KV_DIM),
                   row(KV_DIM), row(KV_DIM), row(IDX_DIM)],
        out_shape=[jax.ShapeDtypeStruct((m, D_B), F32), f32_shape(KV_DIM),
                   jax.ShapeDtypeStruct((m, qi_w), F32), f32_shape(IDX_DIM), f32_shape(KV_DIM),
                   jax.ShapeDtypeStruct((m, KV_DIM), BF16), jax.ShapeDtypeStruct((m, KV_DIM), BF16),
                   jax.ShapeDtypeStruct((m, IDX_DIM), BF16)],
        compiler_params=_params(("parallel",)),
        name="qkrope",
    )(z, z, z, z, z, cos_t, sin_t, qg, kg)


def _rope_tables(pos):
    half = HEAD_DIM // 2
    inv = ROPE_THETA ** (-jnp.arange(half, dtype=F32) / half)
    ang = pos.astype(F32)[:, None] * inv[None, :]
    cos = jnp.cos(ang)
    sin = jnp.sin(ang)
    cos64 = jnp.concatenate([cos, cos], axis=1)
    sin64 = jnp.concatenate([-sin, sin], axis=1)
    reps = LANES // HEAD_DIM
    return jnp.tile(cos64, (1, reps)), jnp.tile(sin64, (1, reps))


_SIGN = np.int32(-2 ** 31)
_MAG = np.int32(0x7FFFFFFF)
QK_SCALE = HEAD_DIM ** -0.5 * math.log2(math.e)
SEARCH_UNROLL = 4
SAMPLE_DIGIT_BITS = 3


def _dot_nt(a, b):
    return lax.dot_general(a, b, (((1,), (1,)), ((), ())), preferred_element_type=F32)


def _row_count(mask):
    return jnp.sum(mask.astype(F32), axis=1, keepdims=True)


def _ordered_to_float(t):
    key = t ^ _SIGN
    return lax.bitcast_convert_type(jnp.where(key < 0, key ^ _MAG, key), F32)


def _threshold_bits(sc_ref, kf):
    nq, nl = sc_ref.shape

    def cond(c):
        i, _, cnt_t = c
        return (i < 32) & (jnp.max(jnp.abs(cnt_t - kf)) > 0.0)

    def body(c):
        i, t, cnt_t = c
        for j in range(SEARCH_UNROLL):
            cand = t | jnp.left_shift(jnp.int32(1), 31 - (i + j))
            cnt = _row_count(sc_ref[...] >= _ordered_to_float(cand))
            ok = cnt >= kf
            t = jnp.where(ok, cand, t)
            cnt_t = jnp.where(ok, cnt, cnt_t)
        return i + SEARCH_UNROLL, t, cnt_t

    init = (jnp.int32(0), jnp.zeros((nq, 1), I32), jnp.full((nq, 1), float(nl), F32))
    return lax.while_loop(cond, body, init)[1:]


def _threshold_digits(sc_ref, kf, digit_bits):
    nq, nl = sc_ref.shape
    t = jnp.zeros((nq, 1), I32)
    cnt_t = jnp.full((nq, 1), float(nl), F32)
    pos = 32
    while pos > 0:
        nb = (pos % digit_bits) or digit_bits
        pos -= nb
        digit = jnp.zeros((nq, 1), I32)
        for j in range(1, 2 ** nb):
            cand = t | np.uint32(j << pos).astype(np.int32)
            cnt = _row_count(sc_ref[...] >= _ordered_to_float(cand))
            ok = cnt >= kf
            digit = digit + ok.astype(I32)
            cnt_t = jnp.where(ok, cnt, cnt_t)
        t = t | jnp.left_shift(digit, pos)
    return t, cnt_t


def _select_topk(sc_ref, sel_ref, topk, digit_bits=1):
    nq, nl = sc_ref.shape
    if nl == topk:
        sel_ref[...] = jnp.ones((nq, nl), F32)
        return
    kf = float(topk)
    t, cnt_t = (_threshold_bits(sc_ref, kf) if digit_bits == 1
                else _threshold_digits(sc_ref, kf, digit_bits))
    tau = jnp.where(t == 0, -jnp.inf, _ordered_to_float(t))
    sel_ref[...] = (sc_ref[...] >= tau).astype(F32)
    tie_rows = (cnt_t > kf) & (tau > NEG)

    @pl.when(jnp.max(tie_rows.astype(F32)) > 0.0)
    def _():
        need = kf - _row_count(sc_ref[...] > tau)
        tri = (lax.broadcasted_iota(I32, (LANES, LANES), 0)
               < lax.broadcasted_iota(I32, (LANES, LANES), 1)).astype(BF16)
        run = jnp.zeros((nq, 1), F32)
        for j in range(nl // LANES):
            sl = slice(j * LANES, (j + 1) * LANES)
            sj = sc_ref[:, sl]
            eqj = sj == tau
            rank = jnp.dot(eqj.astype(BF16), tri, preferred_element_type=F32) + run
            sel_ref[:, sl] = ((sj > tau) | (eqj & (rank < need))).astype(F32)
            run = run + _row_count(eqj)


def _dsa_core(q, qi, wi, bgate, qpos, kb_ref, vb_ref, kib_ref, sc_ref, sel_ref, topk):
    nq = q.shape[0]
    nl = kb_ref.shape[0]
    kib = kib_ref[...]
    qis = (qi * (IDX_DIM ** -0.5)).astype(BF16)
    wis = wi * (IDX_HEADS ** -0.5)
    scores = jnp.zeros((nq, nl), F32)
    for h in range(IDX_HEADS):
        s = _dot_nt(qis[:, h * IDX_DIM:(h + 1) * IDX_DIM], kib)
        scores = scores + jnp.maximum(s, 0.0) * wis[:, h:h + 1]
    causal = _lane_iota((nq, nl)) <= qpos
    sc_ref[...] = jnp.where(causal, scores, NEG)
    _select_topk(sc_ref, sel_ref, topk)

    sel = (sel_ref[...] > 0.0) & causal
    lane = _lane_iota((nq, LANES))
    group_w = N_HEADS // N_KV_HEADS
    kb = kb_ref[...]
    vb = vb_ref[...]
    v_lane = _lane_iota(vb.shape) // HEAD_DIM
    vones = [jnp.where(v_lane == g, vb, jnp.ones_like(vb)) for g in range(N_KV_HEADS)]
    qs = q * QK_SCALE
    outs = []
    for h in range(N_HEADS):
        g = h // group_w
        c = (h * HEAD_DIM) // LANES
        x = qs[:, c * LANES:(c + 1) * LANES]
        if (h % 2) != g:
            x = pltpu.roll(x, HEAD_DIM, 1)
        in_g = (lane // HEAD_DIM) == g
        xq = jnp.where(in_g, x, 0.0).astype(BF16)
        s = jnp.where(sel, _dot_nt(xq, kb), NEG)
        mx = jnp.max(s, axis=1, keepdims=True)
        p = jnp.exp2(s - mx)
        o = jnp.dot(p.astype(BF16), vones[g], preferred_element_type=F32)
        den_lane = ((g + 1) % N_KV_HEADS) * HEAD_DIM
        o = o / o[:, den_lane:den_lane + 1]
        o = jnp.where(in_g, o, 0.0)
        if (h % 2) != g:
            o = pltpu.roll(o, HEAD_DIM, 1)
        outs.append(o)
    cols = [outs[2 * c] + outs[2 * c + 1] for c in range(N_HEADS // 2)]
    return jnp.concatenate(cols, axis=1) * _silu(bgate)


def _dsa_prompt_kernel(acc_hbm, q_ref, qi_ref, wi_ref, bg_ref, kb_ref, vb_ref, kib_ref, o_ref, sc_ref, sel_ref,
                       *, topk, q_lo):
    del acc_hbm
    qb = pl.program_id(1) + q_lo
    nq = q_ref.shape[1]
    qpos = qb * nq + lax.broadcasted_iota(I32, (nq, 1), 0)
    o_ref[0] = _dsa_core(q_ref[0], qi_ref[0], wi_ref[0], bg_ref[0], qpos,
                         kb_ref.at[0], vb_ref.at[0], kib_ref.at[0], sc_ref, sel_ref, topk)


def _dsa_prompt_bucket(acc, q3, qi3, z3, kb3, vb3, kib3, nq, q_lo, q_hi):
    b, t, _ = q3.shape
    topk = min(TOPK_MAX, t // 4)
    nl = q_hi * nq
    qi_w = IDX_HEADS * IDX_DIM
    qblk = lambda w: pl.BlockSpec((1, nq, w), lambda bi, i: (bi, i + q_lo, 0))
    zblk = lambda name, w: pl.BlockSpec((1, nq, w), lambda bi, i, c=_col(name, w): (bi, i + q_lo, c))
    keys = lambda w: pl.BlockSpec((1, nl, w), lambda bi, i: (bi, 0, 0))
    return pl.pallas_call(
        functools.partial(_dsa_prompt_kernel, topk=topk, q_lo=q_lo),
        grid=(b, q_hi - q_lo),
        in_specs=[pl.BlockSpec(memory_space=pl.ANY),
                  qblk(D_B), qblk(qi_w), zblk("zwi", LANES), zblk("b_gate", D_B),
                  keys(KV_DIM), keys(KV_DIM), keys(IDX_DIM)],
        out_specs=pl.BlockSpec((1, nq, D_B), lambda bi, i: (bi, i + q_lo, 0)),
        out_shape=jax.ShapeDtypeStruct((b, t, D_B), F32),
        input_output_aliases={0: 0},
        scratch_shapes=[pltpu.VMEM((nq, nl), F32), pltpu.VMEM((nq, nl), F32)],
        compiler_params=_params(("parallel", "parallel")),
        name="dsa_prompt",
    )(acc, q3, qi3, z3, z3, kb3, vb3, kib3)


DSA_BUCKETS = 8
DSA_ROWS = 256


def _dsa_prompt(q3, qi3, z3, kb3, vb3, kib3):
    t = q3.shape[1]
    nq = DSA_ROWS if t % DSA_ROWS == 0 else Q_BLOCK
    nblk = t // nq
    step = max(1, nblk // DSA_BUCKETS)
    acc = jnp.zeros(q3.shape[:2] + (D_B,), F32)
    for lo in range(0, nblk, step):
        acc = _dsa_prompt_bucket(acc, q3, qi3, z3, kb3, vb3, kib3, nq, lo, min(lo + step, nblk))
    return acc


def _dsa_sample_kernel(pt_ref, q_ref, qi_ref, wi_ref, bg_ref, knt_ref, vnt_ref, kint_ref,
                       ck_hbm, cv_hbm, cki_hbm, o_ref, kt_ref, vt_ref, kit_ref, sc_ref, sel_ref, sems,
                       *, topk, n_pages, page_base, ts):
    bi = pl.program_id(0)
    nb = pl.num_programs(0)
    nq = q_ref.shape[1]
    past = n_pages * PAGE_SIZE
    nl = kt_ref.shape[2]
    group_w = N_HEADS // N_KV_HEADS
    slot = bi % 2

    def copies(seq, s, p):
        page = pt_ref[seq, p] + page_base
        dst = pl.ds(p * PAGE_SIZE, PAGE_SIZE)
        return (pltpu.make_async_copy(ck_hbm.at[page], kt_ref.at[s, :, dst], sems.at[s, 0]),
                pltpu.make_async_copy(cv_hbm.at[page], vt_ref.at[s, :, dst], sems.at[s, 1]),
                pltpu.make_async_copy(cki_hbm.at[page], kit_ref.at[s, :, dst], sems.at[s, 2]))

    def start_gather(seq, s):
        for p in range(n_pages):
            for cp in copies(seq, s, p):
                cp.start()

    @pl.when(bi == 0)
    def _():
        start_gather(0, 0)

    @pl.when(bi + 1 < nb)
    def _():
        start_gather(bi + 1, 1 - slot)

    kt_ref[slot, :, past:nl] = knt_ref[0]
    vt_ref[slot, :, past:nl] = vnt_ref[0]
    kit_ref[slot, :, past:nl] = kint_ref[0]

    for p in range(n_pages):
        for cp in copies(bi, slot, p):
            cp.wait()
    kt_ref, vt_ref, kit_ref = kt_ref.at[slot], vt_ref.at[slot], kit_ref.at[slot]

    row = lax.broadcasted_iota(I32, (nq, 1), 0)
    qpos = past + jnp.minimum(row, ts - 1)
    stack = lambda x, w, heads: jnp.concatenate([x[:, h * w:(h + 1) * w] for h in heads], axis=0)

    qis = stack(qi_ref[0] * (IDX_DIM ** -0.5), IDX_DIM, range(IDX_HEADS)).astype(BF16)
    s = jnp.dot(qis, kit_ref[...].astype(BF16), preferred_element_type=F32)
    wis = wi_ref[0] * (IDX_HEADS ** -0.5)
    scores = jnp.zeros((nq, nl), F32)
    for h in range(IDX_HEADS):
        scores = scores + jnp.maximum(s[h * nq:(h + 1) * nq], 0.0) * wis[:, h:h + 1]
    causal = _lane_iota((nq, nl)) <= qpos
    sc_ref[...] = jnp.where(causal, scores, NEG)
    _select_topk(sc_ref, sel_ref, topk, digit_bits=SAMPLE_DIGIT_BITS)

    sel = (sel_ref[...] > 0.0) & causal
    sel_g = jnp.concatenate([sel] * group_w, axis=0)
    qs = q_ref[0] * QK_SCALE
    outs = []
    for g in range(N_KV_HEADS):
        rows = slice(g * HEAD_DIM, (g + 1) * HEAD_DIM)
        qg = stack(qs, HEAD_DIM, range(g * group_w, (g + 1) * group_w)).astype(BF16)
        sg = jnp.dot(qg, kt_ref[rows, :].astype(BF16), preferred_element_type=F32)
        sg = jnp.where(sel_g, sg, NEG)
        mx = jnp.max(sg, axis=1, keepdims=True)
        p = jnp.exp2(sg - mx)
        den = jnp.sum(p, axis=1, keepdims=True)
        og = _dot_nt(p.astype(BF16), vt_ref[rows, :].astype(BF16)) / den
        outs += [og[j * nq:(j + 1) * nq] for j in range(group_w)]
    o_ref[0] = jnp.concatenate(outs, axis=1) * _silu(bg_ref[0])


def _dsa_sample(page_table, q8, qi8, wi8, bg8, knt, vnt, kint, ckt, cvt, ckit, layer, depth, ts):
    b, nq, _ = q8.shape
    n_pages = page_table.shape[1]
    past = n_pages * PAGE_SIZE
    nl = past + LANES
    topk = min(TOPK_MAX, (past + ts) // 4)
    n_pool = ckt.shape[0] // depth
    qi_w = IDX_HEADS * IDX_DIM
    blk = lambda r, w: pl.BlockSpec((1, r, w), lambda bi, pt: (bi, 0, 0))
    anyspec = pl.BlockSpec(memory_space=pl.ANY)
    grid_spec = pltpu.PrefetchScalarGridSpec(
        num_scalar_prefetch=1,
        grid=(b,),
        in_specs=[blk(nq, D_B), blk(nq, qi_w), blk(nq, LANES), blk(nq, D_B),
                  blk(KV_DIM, LANES), blk(KV_DIM, LANES), blk(IDX_DIM, LANES),
                  anyspec, anyspec, anyspec],
        out_specs=pl.BlockSpec((1, nq, D_B), lambda bi, pt: (bi, 0, 0)),
        scratch_shapes=[pltpu.VMEM((2, KV_DIM, nl), F32), pltpu.VMEM((2, KV_DIM, nl), F32),
                        pltpu.VMEM((2, IDX_DIM, nl), F32),
                        pltpu.VMEM((nq, nl), F32), pltpu.VMEM((nq, nl), F32),
                        pltpu.SemaphoreType.DMA((2, 3))])
    return pl.pallas_call(
        functools.partial(_dsa_sample_kernel, topk=topk, n_pages=n_pages, page_base=layer * n_pool, ts=ts),
        grid_spec=grid_spec,
        out_shape=jax.ShapeDtypeStruct((b, nq, D_B), F32),
        compiler_params=_params(("arbitrary",)),
        name="dsa_sample",
    )(page_table, q8, qi8, wi8, bg8, knt, vnt, kint, ckt, cvt, ckit)


def _hgrn_kernel(cq_ref, cf_ref, ci_ref, cg_ref, lb_ref, ng_ref, s0_ref, yc_ref, s_ref, *, c, valid):
    ci_idx = pl.program_id(1)

    @pl.when(ci_idx == 0)
    def _():
        s_ref[...] = s0_ref[...]

    for r in range(cq_ref.shape[0]):
        _hgrn_chunk(r, cq_ref, cf_ref, ci_ref, cg_ref, lb_ref, ng_ref, yc_ref, s_ref, c, valid)


def _hgrn_chunk(r, cq_ref, cf_ref, ci_ref, cg_ref, lb_ref, ng_ref, yc_ref, s_ref, c, valid):
    fx = cf_ref[r]
    lb = lb_ref[...]
    log_f = jnp.log(jnp.maximum(lb, LB_FLOOR) + (1.0 - lb) * _sigmoid(fx))
    kk = (1.0 - lb) * _sigmoid(-fx)
    qc = _silu(cq_ref[r])
    iv = ci_ref[r]
    row = lax.broadcasted_iota(I32, (c, 1), 0)
    if valid < c:
        log_f = jnp.where(row < valid, log_f, 0.0)

    rr = lax.broadcasted_iota(I32, (c, c), 0)
    cc = lax.broadcasted_iota(I32, (c, c), 1)
    hs = [slice(h * C_KDIM, (h + 1) * C_KDIM) for h in range(C_HEADS)]
    att = [jnp.zeros((c, c), F32) for _ in range(C_HEADS)]
    cs = log_f
    tot = log_f
    m = 1
    while m < c:
        right = ((row // m) % 2) == 1
        qm = jnp.where(right, qc * jnp.exp(cs), 0.0).astype(BF16)
        km = jnp.where(right, 0.0, kk * jnp.exp(tot - cs)).astype(BF16)
        pair = (rr // (2 * m)) == (cc // (2 * m))
        for h in range(C_HEADS):
            att[h] = att[h] + jnp.where(pair, _dot_nt(qm[:, hs[h]], km[:, hs[h]]), 0.0)
        sib = jnp.where(right, pltpu.roll(tot, m, 0), pltpu.roll(tot, c - m, 0))
        cs = jnp.where(right, cs + sib, cs)
        tot = tot + sib
        m *= 2
    qdec = (qc * jnp.exp(cs)).astype(BF16)
    kdec = (kk * jnp.exp(tot - cs)).astype(BF16)
    ivb = iv.astype(BF16)
    eye_c = rr == cc
    eye_k = (lax.broadcasted_iota(I32, (C_KDIM, C_KDIM), 0)
             == lax.broadcasted_iota(I32, (C_KDIM, C_KDIM), 1))
    ys = []
    for h in range(C_HEADS):
        sl = hs[h]
        diag = jnp.sum(qc[:, sl] * kk[:, sl], axis=1, keepdims=True)
        a_h = att[h] + jnp.where(eye_c, diag, 0.0)
        s_h = s_ref[r, h]
        o = (jnp.dot(a_h.astype(BF16), ivb[:, sl], preferred_element_type=F32)
             + jnp.dot(qdec[:, sl], s_h.astype(BF16), preferred_element_type=F32))
        e_end = jnp.exp(tot[0:1, sl])
        e_col = jnp.sum(jnp.where(eye_k, e_end, 0.0), axis=1, keepdims=True)
        upd = lax.dot_general(kdec[:, sl], ivb[:, sl], (((0,), (0,)), ((), ())),
                              preferred_element_type=F32)
        s_ref[r, h] = e_col * s_h + upd
        ms = jnp.mean(o * o, axis=-1, keepdims=True)
        ys.append(o * lax.rsqrt(ms + EPS) * ng_ref[...])
    yc_ref[r] = jnp.concatenate(ys, axis=1) * _silu(cg_ref[r])


HGRN_SEQS = 8


def _hgrn(z3, lb, cng, s0, c, valid):
    b, t, _ = z3.shape
    nb = math.gcd(b, HGRN_SEQS)
    zblk = lambda name: pl.BlockSpec((nb, c, D_C), lambda bi, i, col=_col(name, D_C): (bi, i, col))
    sblk = pl.BlockSpec((nb, C_HEADS, C_KDIM, C_VDIM), lambda bi, i: (bi, 0, 0, 0))
    return pl.pallas_call(
        functools.partial(_hgrn_kernel, c=c, valid=valid),
        grid=(b // nb, t // c),
        in_specs=[zblk("cq"), zblk("cf"), zblk("ci"), zblk("c_gate"),
                  pl.BlockSpec((1, C_FDIM), lambda bi, i: (0, 0)),
                  pl.BlockSpec((1, C_VDIM), lambda bi, i: (0, 0)), sblk],
        out_specs=[pl.BlockSpec((nb, c, D_C), lambda bi, i: (bi, i, 0)), sblk],
        out_shape=[jax.ShapeDtypeStruct((b, t, D_C), F32),
                   jax.ShapeDtypeStruct((b, C_HEADS, C_KDIM, C_VDIM), F32)],
        compiler_params=_params(("parallel", "arbitrary")),
        name="hgrn",
    )(z3, z3, z3, z3, lb.reshape(1, C_FDIM), cng.reshape(1, C_VDIM), s0)


def _merge_kernel(x_ref, ya_ref, yb_ref, yc_ref, ga_ref, gb_ref, gc_ref, gate_ref,
                  wa_ref, wb_ref, wc_ref, wo_ref, o_ref):
    def proj(y_ref, w_ref):
        return jnp.dot(y_ref[...].astype(BF16), w_ref[...], preferred_element_type=F32)

    m = (_sigmoid(ga_ref[...]) * proj(ya_ref, wa_ref)
         + _sigmoid(gb_ref[...]) * proj(yb_ref, wb_ref)
         + _sigmoid(gc_ref[...]) * proj(yc_ref, wc_ref))
    o_ref[...] = x_ref[...] + gate_ref[0] * jnp.dot(m.astype(BF16), wo_ref[...], preferred_element_type=F32)


def _merge(x2, ya, yb, yc, z, gate3, wa, wb, wc, wo, tm, tiles_per_mod):
    m, d = x2.shape
    r = gate3.shape[1]
    row = lambda w: pl.BlockSpec((tm, w), lambda i: (i, 0))
    zc = lambda name: pl.BlockSpec((tm, d), lambda i, c=_col(name, d): (i, c))
    wspec = lambda k: pl.BlockSpec((k, d), lambda i: (0, 0))
    return pl.pallas_call(
        _merge_kernel,
        grid=(m // tm,),
        in_specs=[row(d), row(D_A), row(D_B), row(D_C), zc("g_a"), zc("g_b"), zc("g_c"),
                  pl.BlockSpec((1, r, d), lambda i: (i // tiles_per_mod, 0, 0)),
                  wspec(D_A), wspec(D_B), wspec(D_C), wspec(d)],
        out_specs=row(d),
        out_shape=jax.ShapeDtypeStruct((m, d), F32),
        compiler_params=_params(("parallel",)),
        name="merge",
    )(x2, ya, yb, yc, z, z, z, gate3, wa, wb, wc, wo)


def _pick_tile(n, pref):
    t = min(pref, n)
    while n % t:
        t //= 2
    return t


def kernel(x_prompt, x_sample, cache_k, cache_v, cache_idx_k, state_conv, state_hgrn, page_table,
           c_prompt, c_sample, w_ada, b_ada, norm_g, w_in, w_dw, b_dw, ln_g, ln_b, q_norm_g, k_norm_g,
           lb_logits, c_norm_g, w_proj_a, w_proj_b, w_proj_c, w_out):
    depth = w_in.shape[0]
    bp, tp, d = x_prompt.shape
    bs, ts, _ = x_sample.shape
    mp, ms = bp * tp, bs * ts
    n_pages = page_table.shape[1]
    past = n_pages * PAGE_SIZE
    assert d == D_MODEL and tp % Q_BLOCK == 0 and tp % CHUNK == 0 and tp >= HALO
    assert ts <= SUBLANES and ms % SUBLANES == 0

    lbp = jax.nn.softmax(lb_logits.astype(F32), axis=0)
    lb_all = jnp.cumsum(lbp, axis=0) - lbp[0:1]
    cos_p, sin_p = _rope_tables(jnp.arange(tp))
    cos_s, sin_s = _rope_tables(past + (jnp.arange(ms) % ts))
    ckt = jnp.transpose(cache_k, (0, 1, 3, 4, 2)).reshape(-1, KV_DIM, PAGE_SIZE)
    cvt = jnp.transpose(cache_v, (0, 1, 3, 4, 2)).reshape(-1, KV_DIM, PAGE_SIZE)
    ckit = jnp.transpose(cache_idx_k, (0, 1, 3, 2)).reshape(-1, IDX_DIM, PAGE_SIZE)
    c_all = jnp.concatenate([c_prompt, c_sample], axis=0)

    tm_p = _pick_tile(mp, 1024)
    tm_p = _pick_tile(tp, tm_p)
    tm_in = _pick_tile(tp, 2048)
    tt = _pick_tile(tp, 512)
    tm_g = _pick_tile(tp, 512)
    zeros_buf = jnp.zeros((bp, HALO, D_A), F32)
    s00 = jnp.zeros((bp, C_HEADS, C_KDIM, C_VDIM), F32)
    pad_rows = lambda a3: jnp.pad(a3, ((0, 0), (0, SUBLANES - ts), (0, 0)))

    xp = x_prompt.reshape(mp, d)
    xs = x_sample.reshape(ms, d)
    outs = [[] for _ in range(10)]
    for l in range(depth):
        mod = _ada(c_all, w_ada[l].astype(BF16), b_ada[l])
        shift, scale, gate = mod[:, :d], mod[:, d:2 * d], mod[:, 2 * d:]
        w_in_t = _pack_w_in_t(w_in[l]).astype(BF16)
        wa, wb, wc, wo = (w_proj_a[l].astype(BF16), w_proj_b[l].astype(BF16),
                          w_proj_c[l].astype(BF16), w_out[l].astype(BF16))
        qg = jnp.tile(q_norm_g[l], LANES // HEAD_DIM).reshape(1, LANES)
        kg = jnp.tile(k_norm_g[l], LANES // HEAD_DIM).reshape(1, LANES)
        lb = lb_all[l]

        per_seq = lambda a: a[:bp].reshape(bp, 1, d)
        h = _prenorm(xp, norm_g[l], per_seq(scale), per_seq(shift), tm_p, tp // tm_p)
        z = _matmul_nt(h, w_in_t, tm_in, NZ // 6)
        z3 = z.reshape(bp, tp, NZ)
        ya, utail = _conv_prompt(z3, zeros_buf, w_dw[l], b_dw[l], ln_g[l], ln_b[l], tt)
        q, kt, qi, kit, vt, kb, vb, kib = _qkrope(z, cos_p, sin_p, qg, kg, tm_p, tp // tm_p, True)
        r3 = lambda a: a.reshape(bp, tp, a.shape[-1])
        yb = _dsa_prompt(r3(q), r3(qi), z3, r3(kb), r3(vb), r3(kib))
        yc, s_p = _hgrn(z3, lb, c_norm_g[l], s00, CHUNK, CHUNK)
        xp = _merge(xp, ya.reshape(mp, D_A), yb.reshape(mp, D_B), yc.reshape(mp, D_C), z,
                    per_seq(gate), wa, wb, wc, wo, tm_g, tp // tm_g)
        heads_last = lambda a: jnp.transpose(a.reshape(bp, N_KV_HEADS, HEAD_DIM, tp), (0, 3, 1, 2))
        outs[0].append(heads_last(kt))
        outs[1].append(heads_last(vt))
        outs[2].append(jnp.swapaxes(kit, 1, 2))
        outs[3].append(utail[:, HALO - (CONV_W - 1):, :])
        outs[4].append(s_p)

        per_row = lambda a: jnp.repeat(a[bp:], ts, axis=0).reshape(1, ms, d)
        hs = _prenorm(xs, norm_g[l], per_row(scale), per_row(shift), ms, 1)
        zs = _matmul_nt(hs, w_in_t, ms, NZ // 6)
        zs3 = zs.reshape(bs, ts, NZ)
        seg_t = lambda name: jnp.swapaxes(
            zs3[:, :, _PACK_OFF[name]:_PACK_OFF[name] + D_A], 0, 1)
        ya_t, u_t = _conv_sample(seg_t("a_val"), seg_t("a_glu"), seg_t("a_gate"),
                                 jnp.swapaxes(state_conv[l], 0, 1), w_dw[l], b_dw[l], ln_g[l], ln_b[l])
        ya_s = jnp.swapaxes(ya_t, 0, 1).reshape(ms, D_A)
        q, k, qi, ki, v, _, _, _ = _qkrope(zs, cos_s, sin_s, qg, kg, ms, 1, False)
        p8 = lambda a: pad_rows(a.reshape(bs, ts, a.shape[-1]))
        zs8 = pad_rows(zs3)
        seg8 = lambda name, w: zs8[:, :, _PACK_OFF[name]:_PACK_OFF[name] + w]
        new_t = lambda a: jnp.pad(jnp.swapaxes(a.reshape(bs, ts, a.shape[-1]), 1, 2),
                                  ((0, 0), (0, 0), (0, LANES - ts)))
        yb8 = _dsa_sample(page_table, p8(q), p8(qi), seg8("zwi", LANES), seg8("b_gate", D_B),
                          new_t(k), new_t(v), new_t(ki), ckt, cvt, ckit, l, depth, ts)
        yc8, s_s = _hgrn(zs8, lb, c_norm_g[l], state_hgrn[l], SUBLANES, ts)
        xs = _merge(xs, ya_s, yb8[:, :ts].reshape(ms, D_B), yc8[:, :ts].reshape(ms, D_C), zs,
                    per_row(gate), wa, wb, wc, wo, ms, 1)
        outs[5].append(k.reshape(bs, ts, N_KV_HEADS, HEAD_DIM))
        outs[6].append(v.reshape(bs, ts, N_KV_HEADS, HEAD_DIM))
        outs[7].append(ki.reshape(bs, ts, IDX_DIM))
        outs[8].append(jnp.concatenate([state_conv[l][:, ts:], jnp.swapaxes(u_t, 0, 1)], axis=1))
        outs[9].append(s_s)

    st = [jnp.stack(o) for o in outs]
    return (xp.reshape(bp, tp, d), xs.reshape(bs, ts, d), st[0], st[1], st[2], st[3], st[4],
            st[5], st[6], st[7], st[8], st[9])
```
